```python
import math
import jax, jax.numpy as jnp
from jax import lax
import numpy as np

D_MODEL = 1024
BATCH = 32
SEQ = 2048
DEPTH = 4
DEC_BATCH = 16
DEC_SEQ = 16
PAST_LEN = 4096

CHUNK = 64
N_HEADS = 16
HEAD_DIM = D_MODEL // N_HEADS
SB_BLOCK = 128
POOL_WINDOWS = (2, 4, 8, 16)
N_POOL_GROUPS = len(POOL_WINDOWS)
POOL_GROUP = D_MODEL // N_POOL_GROUPS
POOL_MAX = max(POOL_WINDOWS)
D_FF = ((8 * D_MODEL // 3 + 255) // 256) * 256
PLE_DIM = 256
N_POOL = (DEPTH + 1) // 2
N_SB = DEPTH // 2
ALPHA = (2.0 * DEPTH) ** 0.25
BETA = (8.0 * DEPTH) ** -0.25
LN_EPS = 1e-5
SB_SCALE = 1.0 / math.sqrt(HEAD_DIM)

kernel_name = "streaming_pool_stickbreak_hybrid"


def layer_norm(x, g, b):
    xf = x.astype(jnp.float32)
    mu = jnp.mean(xf, axis=-1, keepdims=True)
    var = jnp.mean(jnp.square(xf - mu), axis=-1, keepdims=True)
    y = (xf - mu) * lax.rsqrt(var + LN_EPS)
    return (y * g.astype(jnp.float32) + b.astype(jnp.float32)).astype(x.dtype)


def swiglu(x, w1, w3, w2):
    return (jax.nn.silu(x @ w1) * (x @ w3)) @ w2


def pool_mixer(x, prev, pos0, w_grp, scale):
    B, S, D = x.shape
    xe = jnp.concatenate([prev.astype(x.dtype), x], axis=1)
    cs = lax.cumsum(xe.astype(jnp.float32), axis=1)
    cs = jnp.concatenate([jnp.zeros((B, 1, D), jnp.float32), cs], axis=1)
    end = cs[:, POOL_MAX:]
    pos = pos0 + jnp.arange(S)
    outs = []
    for g, w in enumerate(POOL_WINDOWS):
        sl = slice(g * POOL_GROUP, (g + 1) * POOL_GROUP)
        start = cs[:, POOL_MAX - w:POOL_MAX - w + S, sl]
        cnt = jnp.minimum(w, pos + 1).astype(jnp.float32)
        mean = (end[..., sl] - start) / cnt[None, :, None]
        outs.append(mean - x[..., sl].astype(jnp.float32))
    y = jnp.stack(outs, axis=2).astype(x.dtype)
    y = jnp.einsum('bsgc,gcd->bsgd', y, w_grp).reshape(B, S, D) * scale
    new_prev = xe[:, -(POOL_MAX - 1):]
    return y, new_prev


def sb_core(q, k, v, q_pos, k_pos):
    z = jnp.einsum('bqhd,bkhd->bhqk', q, k, preferred_element_type=jnp.float32) * SB_SCALE
    mask = k_pos[None, :] < q_pos[:, None]
    log_keep = jnp.where(mask, jax.nn.log_sigmoid(-z), 0.0)
    after = lax.cumsum(log_keep, axis=3, reverse=True) - log_keep
    w = jnp.where(mask, jnp.exp(jax.nn.log_sigmoid(z) + after), 0.0)
    return jnp.einsum('bhqk,bkhd->bqhd', w.astype(v.dtype), v)


def sb_qkv(x, w_in):
    B, S, _ = x.shape
    qkv = (x @ w_in).reshape(B, S, 3, N_HEADS, HEAD_DIM)
    return qkv[:, :, 0], qkv[:, :, 1], qkv[:, :, 2]


def sb_prompt(x, w_in, w_out):
    B, S, D = x.shape
    q, k, v = sb_qkv(x, w_in)
    nblk = S // SB_BLOCK
    qb = q.reshape(B, nblk, SB_BLOCK, N_HEADS, HEAD_DIM).transpose(1, 0, 2, 3, 4)
    pb = jnp.arange(S).reshape(nblk, SB_BLOCK)
    kpos = jnp.arange(S)
    o = lax.map(lambda a: sb_core(a[0], k, v, a[1], kpos), (qb, pb))
    o = o.transpose(1, 0, 2, 3, 4).reshape(B, S, D)
    return o @ w_out, k, v


def sb_sample(x, k_cache, v_cache, w_in, w_out):
    B, S, D = x.shape
    q, k, v = sb_qkv(x, w_in)
    k_all = jnp.concatenate([k_cache.astype(k.dtype), k], axis=1)
    v_all = jnp.concatenate([v_cache.astype(v.dtype), v], axis=1)
    past = k_cache.shape[1]
    q_pos = past + jnp.arange(S)
    k_pos = jnp.arange(past + S)
    o = sb_core(q, k_all, v_all, q_pos, k_pos).reshape(B, S, D)
    return o @ w_out, k, v


def setup_inputs(seed: int = 0) -> dict:
    key = jax.random.key(seed)
    ks = jax.random.split(key, 20)
    nrm = jax.random.normal
    f32 = jnp.float32
    return {
        "x_prompt": nrm(ks[0], (BATCH, SEQ, D_MODEL), f32),
        "x_sample": nrm(ks[1], (DEC_BATCH, DEC_SEQ, D_MODEL), f32),
        "cache_k": nrm(ks[2], (N_SB, DEC_BATCH, PAST_LEN, N_HEADS, HEAD_DIM), f32),
        "cache_v": nrm(ks[3], (N_SB, DEC_BATCH, PAST_LEN, N_HEADS, HEAD_DIM), f32),
        "state_pool": nrm(ks[4], (N_POOL, DEC_BATCH, POOL_MAX - 1, D_MODEL), f32),
        "p_prompt": nrm(ks[5], (DEPTH, BATCH, SEQ, PLE_DIM), f32),
        "p_sample": nrm(ks[6], (DEPTH, DEC_BATCH, DEC_SEQ, PLE_DIM), f32),
        "ln_g": 1.0 + 0.02 * nrm(ks[7], (DEPTH, 4, D_MODEL), f32),
        "ln_b": 0.02 * nrm(ks[8], (DEPTH, 4, D_MODEL), f32),
        "ffn_w1": nrm(ks[9], (DEPTH, 2, D_MODEL, D_FF), f32) * D_MODEL ** -0.5,
        "ffn_w3": nrm(ks[10], (DEPTH, 2, D_MODEL, D_FF), f32) * D_MODEL ** -0.5,
        "ffn_w2": nrm(ks[11], (DEPTH, 2, D_FF, D_MODEL), f32) * (D_FF ** -0.5 * BETA),
        "pool_w": nrm(ks[12], (N_POOL, N_POOL_GROUPS, POOL_GROUP, POOL_GROUP), f32) * (POOL_GROUP ** -0.5 * BETA),
        "pool_scale": 1.0 + 0.02 * nrm(ks[13], (N_POOL, D_MODEL), f32),
        "sb_w_in": nrm(ks[14], (N_SB, D_MODEL, 3 * D_MODEL), f32) * D_MODEL ** -0.5,
        "sb_w_out": nrm(ks[15], (N_SB, D_MODEL, D_MODEL), f32) * (D_MODEL ** -0.5 * BETA),
        "ple_w_gate": nrm(ks[16], (DEPTH, D_MODEL, D_MODEL), f32) * D_MODEL ** -0.5,
        "ple_b_gate": 0.02 * nrm(ks[17], (DEPTH, D_MODEL), f32),
        "ple_w_proj": nrm(ks[18], (DEPTH, PLE_DIM, D_MODEL), f32) * (PLE_DIM ** -0.5 * BETA),
    }


def reference(x_prompt, x_sample, cache_k, cache_v, state_pool, p_prompt, p_sample,
              ln_g, ln_b, ffn_w1, ffn_w3, ffn_w2, pool_w, pool_scale,
              sb_w_in, sb_w_out, ple_w_gate, ple_b_gate, ple_w_proj):
    def half_ffn(x, i, n):
        h = 0.5 * swiglu(x, ffn_w1[i, n], ffn_w3[i, n], ffn_w2[i, n])
        return layer_norm(ALPHA * x + h, ln_g[i, 2 * n], ln_b[i, 2 * n])

    def post_mixer(x, m, p_i, i):
        x = layer_norm(ALPHA * x + m, ln_g[i, 1], ln_b[i, 1])
        x = half_ffn(x, i, 1)
        gate = jax.nn.sigmoid(x @ ple_w_gate[i] + ple_b_gate[i])
        e = p_i @ ple_w_proj[i]
        return layer_norm(ALPHA * x + gate * e, ln_g[i, 3], ln_b[i, 3])

    xp, xs = x_prompt, x_sample
    kp_l, vp_l, ks_l, vs_l, pp_l, ps_l = [], [], [], [], [], []
    for i in range(DEPTH):
        xp = half_ffn(xp, i, 0)
        xs = half_ffn(xs, i, 0)
        j = i // 2
        if i % 2 == 0:
            prev0 = jnp.zeros((xp.shape[0], POOL_MAX - 1, D_MODEL), xp.dtype)
            mp, np_ = pool_mixer(xp, prev0, 0, pool_w[j], pool_scale[j])
            ms, ns_ = pool_mixer(xs, state_pool[j], state_pool.shape[2] * 0 + PAST_LEN, pool_w[j], pool_scale[j])
            pp_l.append(np_)
            ps_l.append(ns_)
        else:
            mp, kp, vp = sb_prompt(xp, sb_w_in[j], sb_w_out[j])
            ms, kn, vn = sb_sample(xs, cache_k[j], cache_v[j], sb_w_in[j], sb_w_out[j])
            kp_l.append(kp); vp_l.append(vp); ks_l.append(kn); vs_l.append(vn)
        xp = post_mixer(xp, mp, p_prompt[i], i)
        xs = post_mixer(xs, ms, p_sample[i], i)

    new_k_prompt = jnp.stack(kp_l)
    new_v_prompt = jnp.stack(vp_l)
    new_k_sample = jnp.stack(ks_l)
    new_v_sample = jnp.stack(vs_l)
    new_pool_prompt = jnp.stack(pp_l)
    new_pool_sample = jnp.stack(ps_l)
    return (xp, xs, new_k_prompt, new_v_prompt, new_k_sample, new_v_sample, new_pool_prompt, new_pool_sample)
```

```python
import functools
import math

import jax
import jax.numpy as jnp
from jax import lax
from jax.experimental import pallas as pl
from jax.experimental.pallas import tpu as pltpu

N_HEADS = 16
POOL_WINDOWS = (2, 4, 8, 16)
POOL_MAX = max(POOL_WINDOWS)
LN_EPS = 1e-5

LANES = 128
VMEM_LIMIT = 52 * 1024 * 1024
ROW_TILE = 512
FF_CHUNK = 512
ATT_TILE = 256

BF16 = jnp.bfloat16
F32 = jnp.float32


def _dot(a, b):
    return jnp.dot(a, b, preferred_element_type=F32)


def _dot_nt(a, b):
    return lax.dot_general(a, b, (((1,), (1,)), ((), ())), preferred_element_type=F32)


def _ln(x, g, b):
    mu = jnp.mean(x, axis=-1, keepdims=True)
    xc = x - mu
    var = jnp.mean(xc * xc, axis=-1, keepdims=True)
    return xc * lax.rsqrt(var + LN_EPS) * g + b


def _half_ffn(x, w1_ref, w3_ref, w2_ref, g, b, alpha):
    d_ff = w1_ref.shape[1]
    xb = x.astype(BF16)
    acc = None
    for c0 in range(0, d_ff, FF_CHUNK):
        c1 = min(c0 + FF_CHUNK, d_ff)
        a = _dot(xb, w1_ref[:, c0:c1])
        u = _dot(xb, w3_ref[:, c0:c1])
        h = (a * jax.nn.sigmoid(a) * u).astype(BF16)
        part = _dot(h, w2_ref[c0:c1, :])
        acc = part if acc is None else acc + part
    return _ln(alpha * x + 0.5 * acc, g, b)


def _pre_kernel(x_ref, w1_ref, w3_ref, w2_ref, g_ref, b_ref, o_ref, *, alpha):
    o_ref[...] = _half_ffn(x_ref[...], w1_ref, w3_ref, w2_ref, g_ref[0:1, :], b_ref[0:1, :], alpha)


def _pre_qkv_kernel(x_ref, w1_ref, w3_ref, w2_ref, g_ref, b_ref, win_ref, *refs, alpha, sb_scale):
    x1_ref, q_ref, k_ref, v_ref = refs[-4:]
    d = x_ref.shape[1]
    x1 = _half_ffn(x_ref[...], w1_ref, w3_ref, w2_ref, g_ref[0:1, :], b_ref[0:1, :], alpha)
    x1_ref[...] = x1
    xb = x1.astype(BF16)
    q_ref[...] = (_dot(xb, win_ref[:, 0:d]) * sb_scale).astype(BF16)
    k_ref[...] = _dot(xb, win_ref[:, d:2 * d])
    v_ref[...] = _dot(xb, win_ref[:, 2 * d:3 * d])


def _post_kernel(*refs, alpha, has_wout):
    if has_wout:
        (x1_ref, m_ref, p_ref, wout_ref, g_ref, b_ref, w1_ref, w3_ref, w2_ref,
         wg_ref, bg_ref, wp_ref, o_ref) = refs
        m = _dot(m_ref[...], wout_ref[...])
    else:
        (x1_ref, m_ref, p_ref, g_ref, b_ref, w1_ref, w3_ref, w2_ref,
         wg_ref, bg_ref, wp_ref, o_ref) = refs
        m = m_ref[...]
    x2 = _ln(alpha * x1_ref[...] + m, g_ref[1:2, :], b_ref[1:2, :])
    x3 = _half_ffn(x2, w1_ref, w3_ref, w2_ref, g_ref[2:3, :], b_ref[2:3, :], alpha)
    gate = jax.nn.sigmoid(_dot(x3.astype(BF16), wg_ref[...]) + bg_ref[...])
    e = _dot(p_ref[...].astype(BF16), wp_ref[...])
    o_ref[...] = _ln(alpha * x3 + gate * e, g_ref[3:4, :], b_ref[3:4, :])


def _const_spec(shape):
    nd = len(shape)
    return pl.BlockSpec(shape, lambda *_: (0,) * nd, pipeline_mode=pl.Buffered(1))


def _row_spec(tm, width):
    return pl.BlockSpec((tm, width), lambda i: (i, 0))


def _row_params():
    return pltpu.CompilerParams(dimension_semantics=("arbitrary",), vmem_limit_bytes=VMEM_LIMIT)


def _pre_call(x, w1, w3, w2, g, b, alpha):
    rows, d = x.shape
    tm = min(ROW_TILE, rows)
    return pl.pallas_call(
        functools.partial(_pre_kernel, alpha=alpha),
        grid=(rows // tm,),
        in_specs=[_row_spec(tm, d), _const_spec(w1.shape), _const_spec(w3.shape),
                  _const_spec(w2.shape), _const_spec(g.shape), _const_spec(b.shape)],
        out_specs=_row_spec(tm, d),
        out_shape=jax.ShapeDtypeStruct((rows, d), F32),
        compiler_params=_row_params(),
        name="pre_ffn",
    )(x, w1, w3, w2, g, b)


def _pre_qkv_call(x, w1, w3, w2, g, b, w_in, k_all, v_all, j, n_sb, alpha, sb_scale):
    rows, d = x.shape
    tm = min(ROW_TILE, rows)
    kv_spec = pl.BlockSpec((None, tm, d), lambda i: (j, i, 0))
    args = [x, w1, w3, w2, g, b, w_in]
    specs = [_row_spec(tm, d)] + [_const_spec(a.shape) for a in args[1:]]
    aliases = {}
    if k_all is not None:
        aliases = {len(args): 2, len(args) + 1: 3}
        args += [k_all, v_all]
        specs += [pl.BlockSpec(memory_space=pl.ANY)] * 2
    return pl.pallas_call(
        functools.partial(_pre_qkv_kernel, alpha=alpha, sb_scale=sb_scale),
        grid=(rows // tm,),
        in_specs=specs,
        out_specs=[_row_spec(tm, d), _row_spec(tm, d), kv_spec, kv_spec],
        out_shape=[jax.ShapeDtypeStruct((rows, d), F32), jax.ShapeDtypeStruct((rows, d), BF16),
                   jax.ShapeDtypeStruct((n_sb, rows, d), F32),
                   jax.ShapeDtypeStruct((n_sb, rows, d), F32)],
        input_output_aliases=aliases,
        compiler_params=_row_params(),
        name="pre_ffn_qkv",
    )(*args)


def _post_call(x1, m, p, w_out, g, b, w1, w3, w2, wg, bg, wp, alpha):
    rows, d = x1.shape
    tm = min(ROW_TILE, rows)
    has_wout = w_out is not None
    args = [x1, m, p] + ([w_out] if has_wout else []) + [g, b, w1, w3, w2, wg, bg, wp]
    specs = [_row_spec(tm, d), _row_spec(tm, d), _row_spec(tm, p.shape[1])]
    specs += [_const_spec(a.shape) for a in args[3:]]
    return pl.pallas_call(
        functools.partial(_post_kernel, alpha=alpha, has_wout=has_wout),
        grid=(rows // tm,),
        in_specs=specs,
        out_specs=_row_spec(tm, d),
        out_shape=jax.ShapeDtypeStruct((rows, d), F32),
        compiler_params=_row_params(),
        name="post_mixer",
    )(*args)


def _pool_kernel(x_ref, prev_ref, pw_ref, scale_ref, m_ref, np_ref, xe_ref, *, tm, pos0):
    t = pl.program_id(1)
    halo = POOL_MAX

    @pl.when(t == 0)
    def _():
        xe_ref[0:halo, :] = prev_ref[0]

    @pl.when(t > 0)
    def _():
        xe_ref[0:halo, :] = xe_ref[tm:tm + halo, :]

    x = x_ref[0]
    xe_ref[halo:halo + tm, :] = x
    pos = pos0 + t * tm + lax.broadcasted_iota(jnp.int32, (tm, 1), 0)
    gw = x.shape[1] // len(POOL_WINDOWS)
    for g, w in enumerate(POOL_WINDOWS):
        c0, c1 = g * gw, (g + 1) * gw
        xg = x[:, c0:c1]
        s = xg
        for k in range(1, w):
            s = s + xe_ref[halo - k:halo - k + tm, c0:c1]
        cnt = jnp.minimum(w, pos + 1).astype(F32)
        y = (s / cnt - xg).astype(BF16)
        m_ref[0, :, c0:c1] = _dot(y, pw_ref[g]) * scale_ref[:, c0:c1]
    np_ref[0] = x[tm - (POOL_MAX - 1):tm, :]


def _pool_call(x, prev16, pw, scale, pos0):
    bsz, s, d = x.shape
    tm = min(ROW_TILE, s)
    return pl.pallas_call(
        functools.partial(_pool_kernel, tm=tm, pos0=pos0),
        grid=(bsz, s // tm),
        in_specs=[pl.BlockSpec((1, tm, d), lambda b, t: (b, t, 0)),
                  pl.BlockSpec((1, POOL_MAX, d), lambda b, t: (b, 0, 0)),
                  pl.BlockSpec(pw.shape, lambda b, t: (0, 0, 0)),
                  pl.BlockSpec(scale.shape, lambda b, t: (0, 0))],
        out_specs=[pl.BlockSpec((1, tm, d), lambda b, t: (b, t, 0)),
                   pl.BlockSpec((1, POOL_MAX - 1, d), lambda b, t: (b, 0, 0))],
        out_shape=[jax.ShapeDtypeStruct((bsz, s, d), F32),
                   jax.ShapeDtypeStruct((bsz, POOL_MAX - 1, d), F32)],
        scratch_shapes=[pltpu.VMEM((tm + POOL_MAX, d), F32)],
        compiler_params=pltpu.CompilerParams(dimension_semantics=("arbitrary", "arbitrary"),
                                             vmem_limit_bytes=VMEM_LIMIT),
        name="pool_mixer",
    )(x, prev16, pw, scale)


def _tri_incl(t):
    r = lax.broadcasted_iota(jnp.int32, (t, t), 0)
    c = lax.broadcasted_iota(jnp.int32, (t, t), 1)
    return jnp.where(r >= c, 1.0, 0.0).astype(BF16)


def _sb_block(z, v, tri, carry, mask):
    sp = jnp.maximum(z, 0.0) + jnp.log(1.0 + jnp.exp(-jnp.abs(z)))
    if mask is not None:
        sp = jnp.where(mask, sp, 0.0)
    hi = sp.astype(BF16)
    lo = (sp - hi.astype(F32)).astype(BF16)
    cs = _dot(hi, tri) + _dot(lo, tri)
    w = jnp.exp(z - cs - carry)
    if mask is not None:
        w = jnp.where(mask, w, 0.0)
    return _dot(w.astype(BF16), v), carry + cs[:, 0:1]


def _attn_prompt_kernel(q_ref, k_ref, v_ref, o_ref, kt_ref, vb_ref, acc_ref, *, seq, tile):
    hd = LANES // 2
    nblk = seq // tile
    for j in range(nblk):
        kt_ref[j] = k_ref[0, j * tile:(j + 1) * tile, :].T.astype(BF16)
        vb_ref[j] = v_ref[0, j * tile:(j + 1) * tile, :].astype(BF16)
    tri = _tri_incl(tile)
    lane = lax.broadcasted_iota(jnp.int32, (1, LANES), 1)
    r = lax.broadcasted_iota(jnp.int32, (tile, tile), 0)
    c = lax.broadcasted_iota(jnp.int32, (tile, tile), 1)
    diag_mask = c < r

    def q_body(qi, _):
        q0 = pl.multiple_of(qi * tile, tile)
        q = q_ref[0, pl.ds(q0, tile), :]
        qh = [jnp.where(lane < hd, q, jnp.zeros_like(q)), jnp.where(lane >= hd, q, jnp.zeros_like(q))]
        carries = []
        for h in range(2):
            pv, cr = _sb_block(_dot(qh[h], kt_ref[qi]), vb_ref[qi], tri,
                               jnp.zeros((tile, 1), F32), diag_mask)
            acc_ref[h] = pv
            carries.append(cr)

        def k_body(jj, cs):
            kj = qi - 1 - jj
            out = []
            for h in range(2):
                pv, cr = _sb_block(_dot(qh[h], kt_ref[kj]), vb_ref[kj], tri, cs[h], None)
                acc_ref[h] += pv
                out.append(cr)
            return tuple(out)

        lax.fori_loop(0, qi, k_body, tuple(carries))
        o_ref[0, pl.ds(q0, tile), :] = jnp.where(lane < hd, acc_ref[0], acc_ref[1]).astype(o_ref.dtype)
        return 0

    lax.fori_loop(0, nblk, q_body, 0)


def _attn_prompt_call(q, k_all, v_all, j):
    bsz, s, d = q.shape
    tile = min(ATT_TILE, s)
    nblk = s // tile
    spec = pl.BlockSpec((1, s, LANES), lambda b, h: (b, 0, h))
    kv_spec = pl.BlockSpec((None, 1, s, LANES), lambda b, h: (j, b, 0, h))
    return pl.pallas_call(
        functools.partial(_attn_prompt_kernel, seq=s, tile=tile),
        grid=(bsz, d // LANES),
        in_specs=[spec, kv_spec, kv_spec],
        out_specs=spec,
        out_shape=jax.ShapeDtypeStruct((bsz, s, d), BF16),
        scratch_shapes=[pltpu.VMEM((nblk, LANES, tile), BF16),
                        pltpu.VMEM((nblk, tile, LANES), BF16),
                        pltpu.VMEM((2, tile, LANES), F32)],
        compiler_params=pltpu.CompilerParams(dimension_semantics=("arbitrary", "arbitrary"),
                                             vmem_limit_bytes=VMEM_LIMIT),
        name="sb_attn_prompt",
    )(q, k_all, v_all)


def _attn_sample_kernel(q_ref, kn_ref, vn_ref, ck_ref, cv_ref, o_ref, qbd_ref, acc_ref, carry_ref,
                        *, n_new, tile, blk):
    j = pl.program_id(1)
    d = q_ref.shape[2]
    hd = d // N_HEADS
    m = N_HEADS * n_new
    tri = _tri_incl(tile)

    @pl.when(j == 0)
    def _():
        q = q_ref[0]
        col = lax.broadcasted_iota(jnp.int32, (n_new, d), 1)
        for h in range(N_HEADS):
            qbd_ref[h * n_new:(h + 1) * n_new, :] = jnp.where(col // hd == h, q, jnp.zeros_like(q))
        t_idx = lax.broadcasted_iota(jnp.int32, (m, tile), 0) % n_new
        s_idx = lax.broadcasted_iota(jnp.int32, (m, tile), 1)
        z = _dot_nt(qbd_ref[...], kn_ref[0].astype(BF16))
        pv, cr = _sb_block(z, vn_ref[0].astype(BF16), tri, jnp.zeros((m, 1), F32), s_idx < t_idx)
        acc_ref[...] = pv
        carry_ref[...] = jnp.broadcast_to(cr, carry_ref.shape)

    qbd = qbd_ref[...]
    for i in reversed(range(blk // tile)):
        kb = ck_ref[0, i * tile:(i + 1) * tile, :].astype(BF16)
        vb = cv_ref[0, i * tile:(i + 1) * tile, :].astype(BF16)
        pv, cr = _sb_block(_dot_nt(qbd, kb), vb, tri, carry_ref[:, 0:1], None)
        acc_ref[...] += pv
        carry_ref[...] = jnp.broadcast_to(cr, carry_ref.shape)

    @pl.when(j == pl.num_programs(1) - 1)
    def _():
        col = lax.broadcasted_iota(jnp.int32, (n_new, d), 1)
        o = jnp.zeros((n_new, d), F32)
        for h in range(N_HEADS):
            o = o + jnp.where(col // hd == h, acc_ref[h * n_new:(h + 1) * n_new, :], 0.0)
        o_ref[0] = o.astype(o_ref.dtype)


def _attn_sample_call(q, k_new, v_new, cache_k, cache_v):
    bsz, n_new, d = q.shape
    past = cache_k.shape[1]
    tile = ATT_TILE
    blk = min(2 * tile, past)
    nkb = past // blk
    pad = ((0, 0), (0, tile - n_new), (0, 0))
    kn = jnp.pad(k_new, pad)
    vn = jnp.pad(v_new, pad)
    m = N_HEADS * n_new
    new_spec = pl.BlockSpec((1, tile, d), lambda b, j: (b, 0, 0))
    cache_spec = pl.BlockSpec((1, blk, d), lambda b, j: (b, nkb - 1 - j, 0))
    q_spec = pl.BlockSpec((1, n_new, d), lambda b, j: (b, 0, 0))
    return pl.pallas_call(
        functools.partial(_attn_sample_kernel, n_new=n_new, tile=tile, blk=blk),
        grid=(bsz, nkb),
        in_specs=[q_spec, new_spec, new_spec, cache_spec, cache_spec],
        out_specs=q_spec,
        out_shape=jax.ShapeDtypeStruct((bsz, n_new, d), BF16),
        scratch_shapes=[pltpu.VMEM((m, d), BF16), pltpu.VMEM((m, d), F32),
                        pltpu.VMEM((m, LANES), F32)],
        compiler_params=pltpu.CompilerParams(dimension_semantics=("arbitrary", "arbitrary"),
                                             vmem_limit_bytes=VMEM_LIMIT),
        name="sb_attn_sample",
    )(q, kn, vn, cache_k, cache_v)


def kernel(x_prompt, x_sample, cache_k, cache_v, state_pool, p_prompt, p_sample,
           ln_g, ln_b, ffn_w1, ffn_w3, ffn_w2, pool_w, pool_scale,
           sb_w_in, sb_w_out, ple_w_gate, ple_b_gate, ple_w_proj):
    depth = ln_g.shape[0]
    bp, sp, d = x_prompt.shape
    bs, ss, _ = x_sample.shape
    n_sb, _, past, heads, hd = cache_k.shape
    n_pool = state_pool.shape[0]
    assert heads == N_HEADS and hd * heads == d and LANES == 2 * hd
    alpha = (2.0 * depth) ** 0.25
    sb_scale = 1.0 / math.sqrt(hd)

    w1 = ffn_w1.astype(BF16)
    w3 = ffn_w3.astype(BF16)
    w2 = ffn_w2.astype(BF16)
    pw = pool_w.astype(BF16)
    w_in = sb_w_in.astype(BF16)
    w_out = sb_w_out.astype(BF16)
    wg = ple_w_gate.astype(BF16)
    wp = ple_w_proj.astype(BF16)

    xp = x_prompt.reshape(bp * sp, d)
    xs = x_sample.reshape(bs * ss, d)
    pp = p_prompt.reshape(depth, bp * sp, -1)
    ps = p_sample.reshape(depth, bs * ss, -1)
    ck = cache_k.reshape(n_sb, bs, past, d)
    cv = cache_v.reshape(n_sb, bs, past, d)

    kp_all = vp_all = ks_all = vs_all = None
    pool_p, pool_s = [], []
    zeros_prev = jnp.zeros((bp, POOL_MAX, d), F32)

    for i in range(depth):
        j = i // 2
        g, b = ln_g[i], ln_b[i]
        ffn0 = (w1[i, 0], w3[i, 0], w2[i, 0])
        ffn1 = (w1[i, 1], w3[i, 1], w2[i, 1])
        tail = (g, b) + ffn1 + (wg[i], ple_b_gate[i][None, :], wp[i])
        if i % 2 == 0:
            xp1 = _pre_call(xp, *ffn0, g, b, alpha)
            xs1 = _pre_call(xs, *ffn0, g, b, alpha)
            scale = pool_scale[j][None, :]
            mp, npp = _pool_call(xp1.reshape(bp, sp, d), zeros_prev, pw[j], scale, 0)
            prev_s = jnp.pad(state_pool[j], ((0, 0), (1, 0), (0, 0)))
            ms, nps = _pool_call(xs1.reshape(bs, ss, d), prev_s, pw[j], scale, past)
            pool_p.append(npp)
            pool_s.append(nps)
            xp = _post_call(xp1, mp.reshape(bp * sp, d), pp[i], None, *tail, alpha)
            xs = _post_call(xs1, ms.reshape(bs * ss, d), ps[i], None, *tail, alpha)
        else:
            xp1, qp, kp_all, vp_all = _pre_qkv_call(xp, *ffn0, g, b, w_in[j], kp_all, vp_all, j,
                                                    n_sb, alpha, sb_scale)
            xs1, qs, ks_all, vs_all = _pre_qkv_call(xs, *ffn0, g, b, w_in[j], ks_all, vs_all, j,
                                                    n_sb, alpha, sb_scale)
            op = _attn_prompt_call(qp.reshape(bp, sp, d), kp_all.reshape(n_sb, bp, sp, d),
                                   vp_all.reshape(n_sb, bp, sp, d), j)
            osm = _attn_sample_call(qs.reshape(bs, ss, d), ks_all[j].reshape(bs, ss, d),
                                    vs_all[j].reshape(bs, ss, d), ck[j], cv[j])
            xp = _post_call(xp1, op.reshape(bp * sp, d), pp[i], w_out[j], *tail, alpha)
            xs = _post_call(xs1, osm.reshape(bs * ss, d), ps[i], w_out[j], *tail, alpha)

    return (xp.reshape(bp, sp, d), xs.reshape(bs, ss, d),
            kp_all.reshape(n_sb, bp, sp, heads, hd), vp_all.reshape(n_sb, bp, sp, heads, hd),
            ks_all.reshape(n_sb, bs, ss, heads, hd), vs_all.reshape(n_sb, bs, ss, heads, hd),
            jnp.stack(pool_p), jnp.stack(pool_s))
```

```python
import functools
import math

import jax
import jax.numpy as jnp
from jax import lax
from jax.experimental import pallas as pl
from jax.experimental.pallas import tpu as pltpu

N_HEADS = 16
POOL_WINDOWS = (2, 4, 8, 16)
POOL_MAX = max(POOL_WINDOWS)
LN_EPS = 1e-5

LANES = 128
VMEM_LIMIT = 52 * 1024 * 1024
ROW_TILE = 512
FF_CHUNK = 512
ATT_TILE = 256
MASK_BIAS = -1e30
SKIP_MASS = 104.0
SOFTPLUS_CLAMP = 80.0

BF16 = jnp.bfloat16
F32 = jnp.float32


def _dot(a, b):
    return jnp.dot(a, b, preferred_element_type=F32)


def _dot_nt(a, b):
    return lax.dot_general(a, b, (((1,), (1,)), ((), ())), preferred_element_type=F32)


def _ln(x, g, b):
    mu = jnp.mean(x, axis=-1, keepdims=True)
    xc = x - mu
    var = jnp.mean(xc * xc, axis=-1, keepdims=True)
    return xc * lax.rsqrt(var + LN_EPS) * g + b


def _half_ffn(x, w1_ref, w3_ref, w2_ref, g, b, alpha):
    d_ff = w1_ref.shape[1]
    xb = x.astype(BF16)
    acc = None
    for c0 in range(0, d_ff, FF_CHUNK):
        c1 = min(c0 + FF_CHUNK, d_ff)
        a = _dot(xb, w1_ref[:, c0:c1])
        u = _dot(xb, w3_ref[:, c0:c1])
        h = (a * jax.nn.sigmoid(a) * u).astype(BF16)
        part = _dot(h, w2_ref[c0:c1, :])
        acc = part if acc is None else acc + part
    return _ln(alpha * x + 0.5 * acc, g, b)


def _pre_kernel(x_ref, w1_ref, w3_ref, w2_ref, g_ref, b_ref, o_ref, *, alpha):
    o_ref[...] = _half_ffn(x_ref[...], w1_ref, w3_ref, w2_ref, g_ref[0:1, :], b_ref[0:1, :], alpha)


def _pre_qkv_kernel(x_ref, w1_ref, w3_ref, w2_ref, g_ref, b_ref, win_ref, *refs, alpha, sb_scale):
    x1_ref, q_ref, k_ref, v_ref = refs[-4:]
    d = x_ref.shape[1]
    x1 = _half_ffn(x_ref[...], w1_ref, w3_ref, w2_ref, g_ref[0:1, :], b_ref[0:1, :], alpha)
    x1_ref[...] = x1
    xb = x1.astype(BF16)
    q_ref[...] = (_dot(xb, win_ref[:, 0:d]) * sb_scale).astype(BF16)
    k_ref[...] = _dot(xb, win_ref[:, d:2 * d])
    v_ref[...] = _dot(xb, win_ref[:, 2 * d:3 * d])


def _post_kernel(*refs, alpha, has_wout):
    if has_wout:
        (x1_ref, m_ref, p_ref, wout_ref, g_ref, b_ref, w1_ref, w3_ref, w2_ref,
         wg_ref, bg_ref, wp_ref, o_ref) = refs
        m = _dot(m_ref[...], wout_ref[...])
    else:
        (x1_ref, m_ref, p_ref, g_ref, b_ref, w1_ref, w3_ref, w2_ref,
         wg_ref, bg_ref, wp_ref, o_ref) = refs
        m = m_ref[...]
    x2 = _ln(alpha * x1_ref[...] + m, g_ref[1:2, :], b_ref[1:2, :])
    x3 = _half_ffn(x2, w1_ref, w3_ref, w2_ref, g_ref[2:3, :], b_ref[2:3, :], alpha)
    gate = jax.nn.sigmoid(_dot(x3.astype(BF16), wg_ref[...]) + bg_ref[...])
    e = _dot(p_ref[...].astype(BF16), wp_ref[...])
    o_ref[...] = _ln(alpha * x3 + gate * e, g_ref[3:4, :], b_ref[3:4, :])


def _const_spec(shape):
    nd = len(shape)
    return pl.BlockSpec(shape, lambda *_: (0,) * nd, pipeline_mode=pl.Buffered(1))


def _row_spec(tm, width):
    return pl.BlockSpec((tm, width), lambda i: (i, 0))


def _row_params():
    return pltpu.CompilerParams(dimension_semantics=("arbitrary",), vmem_limit_bytes=VMEM_LIMIT)


def _pre_call(x, w1, w3, w2, g, b, alpha):
    rows, d = x.shape
    tm = min(ROW_TILE, rows)
    return pl.pallas_call(
        functools.partial(_pre_kernel, alpha=alpha),
        grid=(rows // tm,),
        in_specs=[_row_spec(tm, d), _const_spec(w1.shape), _const_spec(w3.shape),
                  _const_spec(w2.shape), _const_spec(g.shape), _const_spec(b.shape)],
        out_specs=_row_spec(tm, d),
        out_shape=jax.ShapeDtypeStruct((rows, d), F32),
        compiler_params=_row_params(),
        name="pre_ffn",
    )(x, w1, w3, w2, g, b)


def _pre_qkv_call(x, w1, w3, w2, g, b, w_in, k_all, v_all, j, n_sb, alpha, sb_scale):
    rows, d = x.shape
    tm = min(ROW_TILE, rows)
    kv_spec = pl.BlockSpec((None, tm, d), lambda i: (j, i, 0))
    args = [x, w1, w3, w2, g, b, w_in]
    specs = [_row_spec(tm, d)] + [_const_spec(a.shape) for a in args[1:]]
    aliases = {}
    if k_all is not None:
        aliases = {len(args): 2, len(args) + 1: 3}
        args += [k_all, v_all]
        specs += [pl.BlockSpec(memory_space=pl.ANY)] * 2
    return pl.pallas_call(
        functools.partial(_pre_qkv_kernel, alpha=alpha, sb_scale=sb_scale),
        grid=(rows // tm,),
        in_specs=specs,
        out_specs=[_row_spec(tm, d), _row_spec(tm, d), kv_spec, kv_spec],
        out_shape=[jax.ShapeDtypeStruct((rows, d), F32), jax.ShapeDtypeStruct((rows, d), BF16),
                   jax.ShapeDtypeStruct((n_sb, rows, d), F32),
                   jax.ShapeDtypeStruct((n_sb, rows, d), F32)],
        input_output_aliases=aliases,
        compiler_params=_row_params(),
        name="pre_ffn_qkv",
    )(*args)


def _post_call(x1, m, p_all, i, w_out, g, b, w1, w3, w2, wg, bg, wp, alpha):
    rows, d = x1.shape
    tm = min(ROW_TILE, rows)
    has_wout = w_out is not None
    args = [x1, m, p_all] + ([w_out] if has_wout else []) + [g, b, w1, w3, w2, wg, bg, wp]
    specs = [_row_spec(tm, d), _row_spec(tm, d),
             pl.BlockSpec((None, tm, p_all.shape[2]), lambda r: (i, r, 0))]
    specs += [_const_spec(a.shape) for a in args[3:]]
    return pl.pallas_call(
        functools.partial(_post_kernel, alpha=alpha, has_wout=has_wout),
        grid=(rows // tm,),
        in_specs=specs,
        out_specs=_row_spec(tm, d),
        out_shape=jax.ShapeDtypeStruct((rows, d), F32),
        compiler_params=_row_params(),
        name="post_mixer",
    )(*args)


def _pool_kernel(x_ref, prev_ref, pw_ref, scale_ref, m_ref, np_ref, xe_ref, *, tm, pos0):
    t = pl.program_id(1)
    halo = POOL_MAX

    @pl.when(t == 0)
    def _():
        xe_ref[0:halo, :] = prev_ref[0]

    @pl.when(t > 0)
    def _():
        xe_ref[0:halo, :] = xe_ref[tm:tm + halo, :]

    x = x_ref[0]
    xe_ref[halo:halo + tm, :] = x
    pos = pos0 + t * tm + lax.broadcasted_iota(jnp.int32, (tm, 1), 0)
    gw = x.shape[1] // len(POOL_WINDOWS)
    for g, w in enumerate(POOL_WINDOWS):
        c0, c1 = g * gw, (g + 1) * gw
        xg = x[:, c0:c1]
        s = xg
        for k in range(1, w):
            s = s + xe_ref[halo - k:halo - k + tm, c0:c1]
        cnt = jnp.minimum(w, pos + 1).astype(F32)
        y = (s / cnt - xg).astype(BF16)
        m_ref[0, :, c0:c1] = _dot(y, pw_ref[g]) * scale_ref[:, c0:c1]
    np_ref[0] = x[tm - (POOL_MAX - 1):tm, :]


def _pool_call(x, prev16, pw, scale, pos0):
    bsz, s, d = x.shape
    tm = min(ROW_TILE, s)
    return pl.pallas_call(
        functools.partial(_pool_kernel, tm=tm, pos0=pos0),
        grid=(bsz, s // tm),
        in_specs=[pl.BlockSpec((1, tm, d), lambda b, t: (b, t, 0)),
                  pl.BlockSpec((1, POOL_MAX, d), lambda b, t: (b, 0, 0)),
                  pl.BlockSpec(pw.shape, lambda b, t: (0, 0, 0)),
                  pl.BlockSpec(scale.shape, lambda b, t: (0, 0))],
        out_specs=[pl.BlockSpec((1, tm, d), lambda b, t: (b, t, 0)),
                   pl.BlockSpec((1, POOL_MAX - 1, d), lambda b, t: (b, 0, 0))],
        out_shape=[jax.ShapeDtypeStruct((bsz, s, d), F32),
                   jax.ShapeDtypeStruct((bsz, POOL_MAX - 1, d), F32)],
        scratch_shapes=[pltpu.VMEM((tm + POOL_MAX, d), F32)],
        compiler_params=pltpu.CompilerParams(dimension_semantics=("arbitrary", "arbitrary"),
                                             vmem_limit_bytes=VMEM_LIMIT),
        name="pool_mixer",
    )(x, prev16, pw, scale)


def _tri_incl(t):
    r = lax.broadcasted_iota(jnp.int32, (t, t), 0)
    c = lax.broadcasted_iota(jnp.int32, (t, t), 1)
    return jnp.where(r >= c, 1.0, 0.0).astype(BF16)


def _softplus(z):
    return jnp.maximum(z, jnp.log(1.0 + jnp.exp(jnp.minimum(z, SOFTPLUS_CLAMP))))


def _sb_block(z, v, tri, carry, mask):
    sp = _softplus(z)
    if mask is not None:
        sp = jnp.where(mask, sp, 0.0)
    hi = sp.astype(BF16)
    lo = (sp - hi.astype(F32)).astype(BF16)
    cs = _dot(hi, tri) + _dot(lo, tri)
    w = jnp.exp(z - cs - carry)
    if mask is not None:
        w = jnp.where(mask, w, 0.0)
    return _dot(w.astype(BF16), v), carry + cs[:, 0:1]


def _head_split(q):
    lane = lax.broadcasted_iota(jnp.int32, (1, LANES), 1)
    zero = jnp.zeros_like(q)
    return [jnp.where(lane < LANES // 2, q, zero), jnp.where(lane >= LANES // 2, q, zero)]


def _attn_prompt_kernel(q_ref, k_ref, v_ref, o_ref, kt_ref, vb_ref, tri_ref, bias_ref,
                        z0_ref, z1_ref, hl0_ref, hl1_ref, w0_ref, w1_ref,
                        carry_ref, tot_ref, acc_ref, *, seq, tile):
    nblk = seq // tile
    assert nblk >= 2
    n_items = 2 * nblk - 1
    sink_blk = nblk
    z_refs, hl_refs, w_refs = (z0_ref, z1_ref), (hl0_ref, hl1_ref), (w0_ref, w1_ref)

    for j in range(nblk):
        kt_ref[j] = k_ref[0, j * tile:(j + 1) * tile, :].T.astype(BF16)
        vb_ref[j] = v_ref[0, j * tile:(j + 1) * tile, :].astype(BF16)
    r = lax.broadcasted_iota(jnp.int32, (tile, tile), 0)
    c = lax.broadcasted_iota(jnp.int32, (tile, tile), 1)
    tri = jnp.where(r >= c, 1.0, 0.0).astype(BF16)
    tri_ref[0:tile, :] = tri
    tri_ref[tile:2 * tile, :] = tri
    bias_ref[...] = jnp.where(c < r, 0.0, MASK_BIAS)

    def item(n, diag):
        qi = jnp.minimum((n + 1) // 2, nblk - 1)
        return qi, (qi if diag else qi - 1)

    def stage_a(n, slot, diag):
        qi, kj = item(n, diag)
        q = q_ref[0, pl.ds(pl.multiple_of(qi * tile, tile), tile), :]
        kt = kt_ref[kj]
        for h, qh in enumerate(_head_split(q)):
            z = _dot(qh, kt)
            if diag:
                z = z + bias_ref[...]
            sp = _softplus(z)
            hi = sp.astype(BF16)
            z_refs[slot][h] = z
            hl_refs[slot][h, :, 0:tile] = hi
            hl_refs[slot][h, :, tile:2 * tile] = (sp - hi.astype(F32)).astype(BF16)

    def stage_b(n, slot, diag):
        qi, _ = item(n, diag)
        tot_idx = jnp.where(n < n_items, qi, sink_blk)
        tri2 = tri_ref[...]
        for h in range(2):
            cs = _dot(hl_refs[slot][h], tri2)
            e = z_refs[slot][h] - cs
            mass = cs[:, 0:1]
            if not diag:
                c_in = carry_ref[h][:, 0:1]
                e = e - c_in
                mass = mass + c_in
            w_refs[slot][h] = jnp.exp(e).astype(BF16)
            mass = jnp.broadcast_to(mass, (tile, LANES))
            if diag:
                carry_ref[h] = mass
            tot_ref[tot_idx, h] = mass

    def stage_c(n, slot, diag):
        qi, kj = item(n, diag)
        v = vb_ref[kj]
        for h in range(2):
            pv = _dot(w_refs[slot][h], v)
            if diag:
                acc_ref[h, qi] = pv
            else:
                acc_ref[h, qi] += pv

    stage_a(0, 0, True)
    stage_b(0, 0, True)
    stage_a(1, 1, True)
    stage_c(0, 0, True)
    stage_b(1, 1, True)
    stage_a(2, 0, False)

    def pair_body(m, _):
        n = 2 * m + 3
        stage_c(n - 2, 1, True)
        stage_b(n - 1, 0, False)
        stage_a(n, 1, True)
        stage_c(n - 1, 0, False)
        stage_b(n, 1, True)
        stage_a(n + 1, 0, False)
        return 0

    lax.fori_loop(0, nblk - 1, pair_body, 0)

    def tail_body(qi, _):
        @pl.when(jnp.min(tot_ref[qi]) < SKIP_MASS)
        def _():
            q = q_ref[0, pl.ds(pl.multiple_of(qi * tile, tile), tile), :]
            qh = _head_split(q)
            tri = tri_ref[0:tile, :]

            def cond(state):
                kj, c0, c1 = state
                return jnp.logical_and(kj >= 0, jnp.minimum(jnp.min(c0), jnp.min(c1)) < SKIP_MASS)

            def body(state):
                kj, c0, c1 = state
                out = []
                for h, cr in enumerate((c0, c1)):
                    pv, cr = _sb_block(_dot(qh[h], kt_ref[kj]), vb_ref[kj], tri, cr, None)
                    acc_ref[h, qi] += pv
                    out.append(cr)
                return (kj - 1, out[0], out[1])

            lax.while_loop(cond, body, (qi - 2, tot_ref[qi, 0][:, 0:1], tot_ref[qi, 1][:, 0:1]))
        return 0

    if nblk > 2:
        @pl.when(jnp.min(tot_ref[2:nblk]) < SKIP_MASS)
        def _():
            lax.fori_loop(2, nblk, tail_body, 0)

    lane = lax.broadcasted_iota(jnp.int32, (1, LANES), 1)
    for j in range(nblk):
        o_ref[0, j * tile:(j + 1) * tile, :] = jnp.where(
            lane < LANES // 2, acc_ref[0, j], acc_ref[1, j]).astype(o_ref.dtype)


def _attn_prompt_call(q, k_all, v_all, j):
    bsz, s, d = q.shape
    tile = min(ATT_TILE, s)
    nblk = s // tile
    spec = pl.BlockSpec((1, s, LANES), lambda b, h: (b, 0, h))
    kv_spec = pl.BlockSpec((None, 1, s, LANES), lambda b, h: (j, b, 0, h))
    return pl.pallas_call(
        functools.partial(_attn_prompt_kernel, seq=s, tile=tile),
        grid=(bsz, d // LANES),
        in_specs=[spec, kv_spec, kv_spec],
        out_specs=spec,
        out_shape=jax.ShapeDtypeStruct((bsz, s, d), BF16),
        scratch_shapes=[pltpu.VMEM((nblk, LANES, tile), BF16),
                        pltpu.VMEM((nblk, tile, LANES), BF16),
                        pltpu.VMEM((2 * tile, tile), BF16),
                        pltpu.VMEM((tile, tile), F32)]
                       + [pltpu.VMEM((2, tile, tile), F32)] * 2
                       + [pltpu.VMEM((2, tile, 2 * tile), BF16)] * 2
                       + [pltpu.VMEM((2, tile, tile), BF16)] * 2
                       + [pltpu.VMEM((2, tile, LANES), F32),
                          pltpu.VMEM((nblk + 1, 2, tile, LANES), F32),
                          pltpu.VMEM((2, nblk, tile, LANES), F32)],
        compiler_params=pltpu.CompilerParams(dimension_semantics=("arbitrary", "arbitrary"),
                                             vmem_limit_bytes=VMEM_LIMIT),
        name="sb_attn_prompt",
    )(q, k_all, v_all)


def _attn_sample_kernel(q_ref, kn_ref, vn_ref, ck_ref, cv_ref, o_ref, qbd_ref, acc_ref, carry_ref,
                        *, n_new, tile, blk):
    j = pl.program_id(1)
    d = q_ref.shape[2]
    hd = d // N_HEADS
    m = N_HEADS * n_new
    tri = _tri_incl(tile)

    @pl.when(j == 0)
    def _():
        q = q_ref[0]
        col = lax.broadcasted_iota(jnp.int32, (n_new, d), 1)
        for h in range(N_HEADS):
            qbd_ref[h * n_new:(h + 1) * n_new, :] = jnp.where(col // hd == h, q, jnp.zeros_like(q))
        t_idx = lax.broadcasted_iota(jnp.int32, (m, tile), 0) % n_new
        s_idx = lax.broadcasted_iota(jnp.int32, (m, tile), 1)
        z = _dot_nt(qbd_ref[...], kn_ref[0].astype(BF16))
        pv, cr = _sb_block(z, vn_ref[0].astype(BF16), tri, jnp.zeros((m, 1), F32), s_idx < t_idx)
        acc_ref[...] = pv
        carry_ref[...] = jnp.broadcast_to(cr, carry_ref.shape)

    qbd = qbd_ref[...]
    for i in reversed(range(blk // tile)):
        kb = ck_ref[0, i * tile:(i + 1) * tile, :].astype(BF16)
        vb = cv_ref[0, i * tile:(i + 1) * tile, :].astype(BF16)
        pv, cr = _sb_block(_dot_nt(qbd, kb), vb, tri, carry_ref[:, 0:1], None)
        acc_ref[...] += pv
        carry_ref[...] = jnp.broadcast_to(cr, carry_ref.shape)

    @pl.when(j == pl.num_programs(1) - 1)
    def _():
        col = lax.broadcasted_iota(jnp.int32, (n_new, d), 1)
        o = jnp.zeros((n_new, d), F32)
        for h in range(N_HEADS):
            o = o + jnp.where(col // hd == h, acc_ref[h * n_new:(h + 1) * n_new, :], 0.0)
        o_ref[0] = o.astype(o_ref.dtype)


def _attn_sample_call(q, k_new, v_new, cache_k, cache_v, j):
    bsz, n_new, d = q.shape
    past = cache_k.shape[2]
    tile = ATT_TILE
    blk = min(2 * tile, past)
    nkb = past // blk
    pad = ((0, 0), (0, tile - n_new), (0, 0))
    kn = jnp.pad(k_new, pad)
    vn = jnp.pad(v_new, pad)
    m = N_HEADS * n_new
    new_spec = pl.BlockSpec((1, tile, d), lambda b, t: (b, 0, 0))
    cache_spec = pl.BlockSpec((None, 1, blk, d), lambda b, t: (j, b, nkb - 1 - t, 0))
    q_spec = pl.BlockSpec((1, n_new, d), lambda b, t: (b, 0, 0))
    return pl.pallas_call(
        functools.partial(_attn_sample_kernel, n_new=n_new, tile=tile, blk=blk),
        grid=(bsz, nkb),
        in_specs=[q_spec, new_spec, new_spec, cache_spec, cache_spec],
        out_specs=q_spec,
        out_shape=jax.ShapeDtypeStruct((bsz, n_new, d), BF16),
        scratch_shapes=[pltpu.VMEM((m, d), BF16), pltpu.VMEM((m, d), F32),
                        pltpu.VMEM((m, LANES), F32)],
        compiler_params=pltpu.CompilerParams(dimension_semantics=("arbitrary", "arbitrary"),
                                             vmem_limit_bytes=VMEM_LIMIT),
        name="sb_attn_sample",
    )(q, kn, vn, cache_k, cache_v)


def kernel(x_prompt, x_sample, cache_k, cache_v, state_pool, p_prompt, p_sample,
           ln_g, ln_b, ffn_w1, ffn_w3, ffn_w2, pool_w, pool_scale,
           sb_w_in, sb_w_out, ple_w_gate, ple_b_gate, ple_w_proj):
    depth = ln_g.shape[0]
    bp, sp, d = x_prompt.shape
    bs, ss, _ = x_sample.shape
    n_sb, _, past, heads, hd = cache_k.shape
    n_pool = state_pool.shape[0]
    assert heads == N_HEADS and hd * heads == d and LANES == 2 * hd
    alpha = (2.0 * depth) ** 0.25
    sb_scale = 1.0 / math.sqrt(hd)

    w1 = ffn_w1.astype(BF16)
    w3 = ffn_w3.astype(BF16)
    w2 = ffn_w2.astype(BF16)
    pw = pool_w.astype(BF16)
    w_in = sb_w_in.astype(BF16)
    w_out = sb_w_out.astype(BF16)
    wg = ple_w_gate.astype(BF16)
    wp = ple_w_proj.astype(BF16)

    xp = x_prompt.reshape(bp * sp, d)
    xs = x_sample.reshape(bs * ss, d)
    pp = p_prompt.reshape(depth, bp * sp, -1)
    ps = p_sample.reshape(depth, bs * ss, -1)
    ck = cache_k.reshape(n_sb, bs, past, d)
    cv = cache_v.reshape(n_sb, bs, past, d)

    kp_all = vp_all = ks_all = vs_all = None
    pool_p, pool_s = [], []
    zeros_prev = jnp.zeros((bp, POOL_MAX, d), F32)

    for i in range(depth):
        j = i // 2
        g, b = ln_g[i], ln_b[i]
        ffn0 = (w1[i, 0], w3[i, 0], w2[i, 0])
        ffn1 = (w1[i, 1], w3[i, 1], w2[i, 1])
        tail = (g, b) + ffn1 + (wg[i], ple_b_gate[i][None, :], wp[i])
        if i % 2 == 0:
            xp1 = _pre_call(xp, *ffn0, g, b, alpha)
            xs1 = _pre_call(xs, *ffn0, g, b, alpha)
            scale = pool_scale[j][None, :]
            mp, npp = _pool_call(xp1.reshape(bp, sp, d), zeros_prev, pw[j], scale, 0)
            prev_s = jnp.pad(state_pool[j], ((0, 0), (1, 0), (0, 0)))
            ms, nps = _pool_call(xs1.reshape(bs, ss, d), prev_s, pw[j], scale, past)
            pool_p.append(npp)
            pool_s.append(nps)
            xp = _post_call(xp1, mp.reshape(bp * sp, d), pp, i, None, *tail, alpha)
            xs = _post_call(xs1, ms.reshape(bs * ss, d), ps, i, None, *tail, alpha)
        else:
            xp1, qp, kp_all, vp_all = _pre_qkv_call(xp, *ffn0, g, b, w_in[j], kp_all, vp_all, j,
                                                    n_sb, alpha, sb_scale)
            xs1, qs, ks_all, vs_all = _pre_qkv_call(xs, *ffn0, g, b, w_in[j], ks_all, vs_all, j,
                                                    n_sb, alpha, sb_scale)
            op = _attn_prompt_call(qp.reshape(bp, sp, d), kp_all.reshape(n_sb, bp, sp, d),
                                   vp_all.reshape(n_sb, bp, sp, d), j)
            osm = _attn_sample_call(qs.reshape(bs, ss, d), ks_all[j].reshape(bs, ss, d),
                                    vs_all[j].reshape(bs, ss, d), ck, cv, j)
            xp = _post_call(xp1, op.reshape(bp * sp, d), pp, i, w_out[j], *tail, alpha)
            xs = _post_call(xs1, osm.reshape(bs * ss, d), ps, i, w_out[j], *tail, alpha)

    return (xp.reshape(bp, sp, d), xs.reshape(bs, ss, d),
            kp_all.reshape(n_sb, bp, sp, heads, hd), vp_all.reshape(n_sb, bp, sp, heads, hd),
            ks_all.reshape(n_sb, bs, ss, heads, hd), vs_all.reshape(n_sb, bs, ss, heads, hd),
            jnp.stack(pool_p), jnp.stack(pool_s))
```

```python
import functools
import math

import jax
import jax.numpy as jnp
from jax import lax
from jax.experimental import pallas as pl
from jax.experimental.pallas import tpu as pltpu

N_HEADS = 16
POOL_WINDOWS = (2, 4, 8, 16)
POOL_MAX = max(POOL_WINDOWS)
LN_EPS = 1e-5

LANES = 128
VMEM_LIMIT = 52 * 1024 * 1024
ROW_TILE = 512
FF_CHUNK = 512
ATT_TILE = 256
MASK_BIAS = -1e30
SKIP_MASS = 104.0
SOFTPLUS_CLAMP = 80.0

BF16 = jnp.bfloat16
F32 = jnp.float32


def _dot(a, b):
    return jnp.dot(a, b, preferred_element_type=F32)


def _dot_nt(a, b):
    return lax.dot_general(a, b, (((1,), (1,)), ((), ())), preferred_element_type=F32)


def _ln(x, g, b):
    mu = jnp.mean(x, axis=-1, keepdims=True)
    xc = x - mu
    var = jnp.mean(xc * xc, axis=-1, keepdims=True)
    return xc * lax.rsqrt(var + LN_EPS) * g + b


def _half_ffn(x, w1_ref, w3_ref, w2_ref, g, b, alpha):
    d_ff = w1_ref.shape[1]
    xb = x.astype(BF16)
    acc = None
    for c0 in range(0, d_ff, FF_CHUNK):
        c1 = min(c0 + FF_CHUNK, d_ff)
        a = _dot(xb, w1_ref[:, c0:c1])
        u = _dot(xb, w3_ref[:, c0:c1])
        h = (a * jax.nn.sigmoid(a) * u).astype(BF16)
        part = _dot(h, w2_ref[c0:c1, :])
        acc = part if acc is None else acc + part
    return _ln(alpha * x + 0.5 * acc, g, b)


def _post_chain(x1, m, p, g_ref, b_ref, w1_ref, w3_ref, w2_ref, wg_ref, bg_ref, wp_ref, alpha):
    x2 = _ln(alpha * x1 + m, g_ref[1:2, :], b_ref[1:2, :])
    x3 = _half_ffn(x2, w1_ref, w3_ref, w2_ref, g_ref[2:3, :], b_ref[2:3, :], alpha)
    gate = jax.nn.sigmoid(_dot(x3.astype(BF16), wg_ref[...]) + bg_ref[...])
    e = _dot(p.astype(BF16), wp_ref[...])
    return _ln(alpha * x3 + gate * e, g_ref[3:4, :], b_ref[3:4, :])


def _pool_mix(x, xe_ref, pos, pw_ref, scale_ref):
    tm = x.shape[0]
    gw = x.shape[1] // len(POOL_WINDOWS)
    parts = []
    for g, w in enumerate(POOL_WINDOWS):
        c0, c1 = g * gw, (g + 1) * gw
        xg = x[:, c0:c1]
        s = xg
        for k in range(1, w):
            s = s + xe_ref[POOL_MAX - k:POOL_MAX - k + tm, c0:c1]
        cnt = jnp.minimum(w, pos + 1).astype(F32)
        y = (s / cnt - xg).astype(BF16)
        parts.append(_dot(y, pw_ref[g]) * scale_ref[:, c0:c1])
    return jnp.concatenate(parts, axis=1)


def _pre_kernel(x_ref, w1_ref, w3_ref, w2_ref, g_ref, b_ref, o_ref, *, alpha):
    o_ref[...] = _half_ffn(x_ref[...], w1_ref, w3_ref, w2_ref, g_ref[0:1, :], b_ref[0:1, :], alpha)


def _pre_qkv_kernel(x_ref, w1_ref, w3_ref, w2_ref, g_ref, b_ref, win_ref, *refs, alpha, sb_scale):
    x1_ref, q_ref, k_ref, v_ref = refs[-4:]
    d = x_ref.shape[1]
    x1 = _half_ffn(x_ref[...], w1_ref, w3_ref, w2_ref, g_ref[0:1, :], b_ref[0:1, :], alpha)
    x1_ref[...] = x1
    xb = x1.astype(BF16)
    q_ref[...] = _dot(xb, win_ref[:, 0:d]) * sb_scale
    k_ref[...] = _dot(xb, win_ref[:, d:2 * d])
    v_ref[...] = _dot(xb, win_ref[:, 2 * d:3 * d])


def _post_kernel(*refs, alpha, has_wout):
    if has_wout:
        x1_ref, m_ref, p_ref, wout_ref = refs[:4]
        m = _dot(m_ref[...].astype(BF16), wout_ref[...])
    else:
        x1_ref, m_ref, p_ref = refs[:3]
        m = m_ref[...]
    o_ref = refs[-1]
    o_ref[...] = _post_chain(x1_ref[...], m, p_ref[...], *refs[-9:-1], alpha)


def _post_pool_kernel(x1_ref, p_ref, pw_ref, scale_ref, g_ref, b_ref, w1_ref, w3_ref, w2_ref,
                      wg_ref, bg_ref, wp_ref, o_ref, np_ref, xe_ref, *, alpha, tiles_per_seq):
    tm = x1_ref.shape[0]
    t = pl.program_id(0) % tiles_per_seq

    @pl.when(t == 0)
    def _():
        xe_ref[0:POOL_MAX, :] = jnp.zeros((POOL_MAX, xe_ref.shape[1]), F32)

    @pl.when(t > 0)
    def _():
        xe_ref[0:POOL_MAX, :] = xe_ref[tm:tm + POOL_MAX, :]

    x1 = x1_ref[...]
    xe_ref[POOL_MAX:POOL_MAX + tm, :] = x1
    pos = t * tm + lax.broadcasted_iota(jnp.int32, (tm, 1), 0)
    m = _pool_mix(x1, xe_ref, pos, pw_ref, scale_ref)
    np_ref[0] = x1[tm - (POOL_MAX - 1):tm, :]
    o_ref[...] = _post_chain(x1, m, p_ref[...], g_ref, b_ref, w1_ref, w3_ref, w2_ref,
                             wg_ref, bg_ref, wp_ref, alpha)


def _const_spec(shape):
    nd = len(shape)
    return pl.BlockSpec(shape, lambda *_: (0,) * nd, pipeline_mode=pl.Buffered(1))


def _row_spec(tm, width):
    return pl.BlockSpec((tm, width), lambda i: (i, 0))


def _row_params():
    return pltpu.CompilerParams(dimension_semantics=("arbitrary",), vmem_limit_bytes=VMEM_LIMIT)


def _pre_call(x, w1, w3, w2, g, b, alpha):
    rows, d = x.shape
    tm = min(ROW_TILE, rows)
    return pl.pallas_call(
        functools.partial(_pre_kernel, alpha=alpha),
        grid=(rows // tm,),
        in_specs=[_row_spec(tm, d), _const_spec(w1.shape), _const_spec(w3.shape),
                  _const_spec(w2.shape), _const_spec(g.shape), _const_spec(b.shape)],
        out_specs=_row_spec(tm, d),
        out_shape=jax.ShapeDtypeStruct((rows, d), F32),
        compiler_params=_row_params(),
        name="pre_ffn",
    )(x, w1, w3, w2, g, b)


def _pre_qkv_call(x, w1, w3, w2, g, b, w_in, k_all, v_all, j, n_sb, alpha, sb_scale):
    rows, d = x.shape
    tm = min(ROW_TILE, rows)
    kv_spec = pl.BlockSpec((None, tm, d), lambda i: (j, i, 0))
    args = [x, w1, w3, w2, g, b, w_in]
    specs = [_row_spec(tm, d)] + [_const_spec(a.shape) for a in args[1:]]
    aliases = {}
    if k_all is not None:
        aliases = {len(args): 2, len(args) + 1: 3}
        args += [k_all, v_all]
        specs += [pl.BlockSpec(memory_space=pl.ANY)] * 2
    return pl.pallas_call(
        functools.partial(_pre_qkv_kernel, alpha=alpha, sb_scale=sb_scale),
        grid=(rows // tm,),
        in_specs=specs,
        out_specs=[_row_spec(tm, d), _row_spec(tm, d), kv_spec, kv_spec],
        out_shape=[jax.ShapeDtypeStruct((rows, d), F32), jax.ShapeDtypeStruct((rows, d), F32),
                   jax.ShapeDtypeStruct((n_sb, rows, d), F32),
                   jax.ShapeDtypeStruct((n_sb, rows, d), F32)],
        input_output_aliases=aliases,
        compiler_params=_row_params(),
        name="pre_ffn_qkv",
    )(*args)


def _post_call(x1, m, p_all, i, w_out, g, b, w1, w3, w2, wg, bg, wp, alpha):
    rows, d = x1.shape
    tm = min(ROW_TILE, rows)
    has_wout = w_out is not None
    args = [x1, m, p_all] + ([w_out] if has_wout else []) + [g, b, w1, w3, w2, wg, bg, wp]
    specs = [_row_spec(tm, d), _row_spec(tm, d),
             pl.BlockSpec((None, tm, p_all.shape[2]), lambda r: (i, r, 0))]
    specs += [_const_spec(a.shape) for a in args[3:]]
    return pl.pallas_call(
        functools.partial(_post_kernel, alpha=alpha, has_wout=has_wout),
        grid=(rows // tm,),
        in_specs=specs,
        out_specs=_row_spec(tm, d),
        out_shape=jax.ShapeDtypeStruct((rows, d), F32),
        compiler_params=_row_params(),
        name="post_mixer",
    )(*args)


def _post_pool_call(x1, seq, p_all, i, pw, scale, g, b, w1, w3, w2, wg, bg, wp, alpha):
    rows, d = x1.shape
    tm = min(ROW_TILE, seq)
    tiles_per_seq = seq // tm
    consts = [pw, scale, g, b, w1, w3, w2, wg, bg, wp]
    specs = [_row_spec(tm, d), pl.BlockSpec((None, tm, p_all.shape[2]), lambda r: (i, r, 0))]
    specs += [_const_spec(a.shape) for a in consts]
    return pl.pallas_call(
        functools.partial(_post_pool_kernel, alpha=alpha, tiles_per_seq=tiles_per_seq),
        grid=(rows // tm,),
        in_specs=specs,
        out_specs=[_row_spec(tm, d),
                   pl.BlockSpec((1, POOL_MAX - 1, d), lambda r: (r // tiles_per_seq, 0, 0))],
        out_shape=[jax.ShapeDtypeStruct((rows, d), F32),
                   jax.ShapeDtypeStruct((rows // seq, POOL_MAX - 1, d), F32)],
        scratch_shapes=[pltpu.VMEM((tm + POOL_MAX, d), F32)],
        compiler_params=_row_params(),
        name="post_pool_mixer",
    )(x1, p_all, *consts)


def _pool_kernel(x_ref, prev_ref, pw_ref, scale_ref, m_ref, np_ref, xe_ref, *, pos0):
    tm = x_ref.shape[1]
    x = x_ref[0]
    xe_ref[0:POOL_MAX, :] = prev_ref[0]
    xe_ref[POOL_MAX:POOL_MAX + tm, :] = x
    pos = pos0 + lax.broadcasted_iota(jnp.int32, (tm, 1), 0)
    m_ref[0] = _pool_mix(x, xe_ref, pos, pw_ref, scale_ref)
    np_ref[0] = x[tm - (POOL_MAX - 1):tm, :]


def _pool_call(x, prev16, pw, scale, pos0):
    bsz, s, d = x.shape
    return pl.pallas_call(
        functools.partial(_pool_kernel, pos0=pos0),
        grid=(bsz,),
        in_specs=[pl.BlockSpec((1, s, d), lambda b: (b, 0, 0)),
                  pl.BlockSpec((1, POOL_MAX, d), lambda b: (b, 0, 0)),
                  pl.BlockSpec(pw.shape, lambda b: (0, 0, 0)),
                  pl.BlockSpec(scale.shape, lambda b: (0, 0))],
        out_specs=[pl.BlockSpec((1, s, d), lambda b: (b, 0, 0)),
                   pl.BlockSpec((1, POOL_MAX - 1, d), lambda b: (b, 0, 0))],
        out_shape=[jax.ShapeDtypeStruct((bsz, s, d), F32),
                   jax.ShapeDtypeStruct((bsz, POOL_MAX - 1, d), F32)],
        scratch_shapes=[pltpu.VMEM((s + POOL_MAX, d), F32)],
        compiler_params=_row_params(),
        name="pool_mixer",
    )(x, prev16, pw, scale)


def _tri_strict(t):
    r = lax.broadcasted_iota(jnp.int32, (t, t), 0)
    c = lax.broadcasted_iota(jnp.int32, (t, t), 1)
    return jnp.where(r > c, 1.0, 0.0)


def _softplus(z):
    return jnp.maximum(z, jnp.log(1.0 + jnp.exp(jnp.minimum(z, SOFTPLUS_CLAMP))))


def _sb_block(z, v, tri, carry, mask):
    sp = _softplus(z)
    ls = z - sp
    if mask is not None:
        sp = jnp.where(mask, sp, 0.0)
    cs = _dot(sp, tri)
    w = jnp.exp(ls - cs - carry)
    if mask is not None:
        w = jnp.where(mask, w, 0.0)
    return _dot(w, v), carry + cs[:, 0:1] + sp[:, 0:1]


def _head_split(q):
    lane = lax.broadcasted_iota(jnp.int32, (1, LANES), 1)
    zero = jnp.zeros_like(q)
    return [jnp.where(lane < LANES // 2, q, zero), jnp.where(lane >= LANES // 2, q, zero)]


def _head_merge(o0, o1):
    lane = lax.broadcasted_iota(jnp.int32, (1, LANES), 1)
    return jnp.where(lane < LANES // 2, o0, o1)


def _attn_prompt_kernel(q_ref, k_ref, v_ref, o_ref, tri_ref, bias_ref,
                        ls0_ref, ls1_ref, sp0_ref, sp1_ref, w0_ref, w1_ref,
                        carry_ref, tot_ref, *, seq, tile):
    nblk = seq // tile
    assert nblk >= 2
    n_items = 2 * nblk - 1
    sink_blk = nblk
    ls_refs, sp_refs, w_refs = (ls0_ref, ls1_ref), (sp0_ref, sp1_ref), (w0_ref, w1_ref)

    @pl.when((pl.program_id(0) == 0) & (pl.program_id(1) == 0))
    def _():
        r = lax.broadcasted_iota(jnp.int32, (tile, tile), 0)
        c = lax.broadcasted_iota(jnp.int32, (tile, tile), 1)
        tri_ref[...] = _tri_strict(tile)
        bias_ref[...] = jnp.where(c < r, 0.0, MASK_BIAS)

    def rows(i):
        return pl.ds(pl.multiple_of(i * tile, tile), tile)

    def item(n, diag):
        qi = jnp.minimum((n + 1) // 2, nblk - 1)
        return qi, (qi if diag else qi - 1)

    def stage_a(n, slot, diag):
        qi, kj = item(n, diag)
        k = k_ref[0, rows(kj), :]
        for h, qh in enumerate(_head_split(q_ref[0, rows(qi), :])):
            z = _dot_nt(qh, k)
            if diag:
                z = z + bias_ref[...]
            sp = _softplus(z)
            sp_refs[slot][h] = sp
            ls_refs[slot][h] = z - sp

    def stage_b(n, slot, diag):
        qi, _ = item(n, diag)
        tot_idx = jnp.where(n < n_items, qi, sink_blk)
        tri = tri_ref[...]
        for h in range(2):
            sp = sp_refs[slot][h]
            cs = _dot(sp, tri)
            e = ls_refs[slot][h] - cs
            mass = cs[:, 0:1] + sp[:, 0:1]
            if not diag:
                c_in = carry_ref[h][:, 0:1]
                e = e - c_in
                mass = mass + c_in
            w_refs[slot][h] = jnp.exp(e)
            mass = jnp.broadcast_to(mass, (tile, LANES))
            if diag:
                carry_ref[h] = mass
            tot_ref[tot_idx, h] = mass

    def stage_c(n, slot, diag):
        qi, kj = item(n, diag)
        v = v_ref[0, rows(kj), :]
        pv = _head_merge(_dot(w_refs[slot][0], v), _dot(w_refs[slot][1], v))
        if diag:
            o_ref[0, rows(qi), :] = pv
        else:
            o_ref[0, rows(qi), :] += pv

    stage_a(0, 0, True)
    stage_b(0, 0, True)
    stage_a(1, 1, True)
    stage_c(0, 0, True)
    stage_b(1, 1, True)
    stage_a(2, 0, False)

    def pair_body(m, _):
        n = 2 * m + 3
        stage_c(n - 2, 1, True)
        stage_b(n - 1, 0, False)
        stage_a(n, 1, True)
        stage_c(n - 1, 0, False)
        stage_b(n, 1, True)
        stage_a(n + 1, 0, False)
        return 0

    lax.fori_loop(0, nblk - 1, pair_body, 0)

    def tail_body(qi, _):
        @pl.when(jnp.min(tot_ref[qi]) < SKIP_MASS)
        def _():
            qh = _head_split(q_ref[0, rows(qi), :])
            tri = tri_ref[...]

            def cond(state):
                kj, c0, c1 = state
                return jnp.logical_and(kj >= 0, jnp.minimum(jnp.min(c0), jnp.min(c1)) < SKIP_MASS)

            def body(state):
                kj, c0, c1 = state
                k = k_ref[0, rows(kj), :]
                v = v_ref[0, rows(kj), :]
                pv0, c0 = _sb_block(_dot_nt(qh[0], k), v, tri, c0, None)
                pv1, c1 = _sb_block(_dot_nt(qh[1], k), v, tri, c1, None)
                o_ref[0, rows(qi), :] += _head_merge(pv0, pv1)
                return (kj - 1, c0, c1)

            lax.while_loop(cond, body, (qi - 2, tot_ref[qi, 0][:, 0:1], tot_ref[qi, 1][:, 0:1]))
        return 0

    if nblk > 2:
        @pl.when(jnp.min(tot_ref[2:nblk]) < SKIP_MASS)
        def _():
            lax.fori_loop(2, nblk, tail_body, 0)


def _attn_prompt_call(q, k_all, v_all, j):
    bsz, s, d = q.shape
    tile = min(ATT_TILE, s)
    nblk = s // tile
    spec = pl.BlockSpec((1, s, LANES), lambda b, h: (b, 0, h))
    kv_spec = pl.BlockSpec((None, 1, s, LANES), lambda b, h: (j, b, 0, h))
    return pl.pallas_call(
        functools.partial(_attn_prompt_kernel, seq=s, tile=tile),
        grid=(bsz, d // LANES),
        in_specs=[spec, kv_spec, kv_spec],
        out_specs=spec,
        out_shape=jax.ShapeDtypeStruct((bsz, s, d), F32),
        scratch_shapes=[pltpu.VMEM((tile, tile), F32),
                        pltpu.VMEM((tile, tile), F32)]
                       + [pltpu.VMEM((2, tile, tile), F32)] * 6
                       + [pltpu.VMEM((2, tile, LANES), F32),
                          pltpu.VMEM((nblk + 1, 2, tile, LANES), F32)],
        compiler_params=pltpu.CompilerParams(dimension_semantics=("arbitrary", "arbitrary"),
                                             vmem_limit_bytes=VMEM_LIMIT),
        name="sb_attn_prompt",
    )(q, k_all, v_all)


def _attn_sample_kernel(q_ref, kn_ref, vn_ref, ck_ref, cv_ref, o_ref, qbd_ref, acc_ref, carry_ref,
                        *, n_new, tile, blk):
    j = pl.program_id(1)
    d = q_ref.shape[2]
    hd = d // N_HEADS
    m = N_HEADS * n_new
    tri = _tri_strict(tile)

    @pl.when(j == 0)
    def _():
        q = q_ref[0]
        col = lax.broadcasted_iota(jnp.int32, (n_new, d), 1)
        for h in range(N_HEADS):
            qbd_ref[h * n_new:(h + 1) * n_new, :] = jnp.where(col // hd == h, q, jnp.zeros_like(q))
        t_idx = lax.broadcasted_iota(jnp.int32, (m, tile), 0) % n_new
        s_idx = lax.broadcasted_iota(jnp.int32, (m, tile), 1)
        z = _dot_nt(qbd_ref[...], kn_ref[0])
        pv, cr = _sb_block(z, vn_ref[0], tri, jnp.zeros((m, 1), F32), s_idx < t_idx)
        acc_ref[...] = pv
        carry_ref[...] = jnp.broadcast_to(cr, carry_ref.shape)

    qbd = qbd_ref[...]
    for i in reversed(range(blk // tile)):
        kb = ck_ref[0, i * tile:(i + 1) * tile, :]
        vb = cv_ref[0, i * tile:(i + 1) * tile, :]
        pv, cr = _sb_block(_dot_nt(qbd, kb), vb, tri, carry_ref[:, 0:1], None)
        acc_ref[...] += pv
        carry_ref[...] = jnp.broadcast_to(cr, carry_ref.shape)

    @pl.when(j == pl.num_programs(1) - 1)
    def _():
        col = lax.broadcasted_iota(jnp.int32, (n_new, d), 1)
        o = jnp.zeros((n_new, d), F32)
        for h in range(N_HEADS):
            o = o + jnp.where(col // hd == h, acc_ref[h * n_new:(h + 1) * n_new, :], 0.0)
        o_ref[0] = o


def _attn_sample_call(q, k_new, v_new, cache_k, cache_v, j):
    bsz, n_new, d = q.shape
    past = cache_k.shape[2]
    tile = ATT_TILE
    blk = min(2 * tile, past)
    nkb = past // blk
    pad = ((0, 0), (0, tile - n_new), (0, 0))
    kn = jnp.pad(k_new, pad)
    vn = jnp.pad(v_new, pad)
    m = N_HEADS * n_new
    new_spec = pl.BlockSpec((1, tile, d), lambda b, t: (b, 0, 0))
    cache_spec = pl.BlockSpec((None, 1, blk, d), lambda b, t: (j, b, nkb - 1 - t, 0))
    q_spec = pl.BlockSpec((1, n_new, d), lambda b, t: (b, 0, 0))
    return pl.pallas_call(
        functools.partial(_attn_sample_kernel, n_new=n_new, tile=tile, blk=blk),
        grid=(bsz, nkb),
        in_specs=[q_spec, new_spec, new_spec, cache_spec, cache_spec],
        out_specs=q_spec,
        out_shape=jax.ShapeDtypeStruct((bsz, n_new, d), F32),
        scratch_shapes=[pltpu.VMEM((m, d), F32), pltpu.VMEM((m, d), F32),
                        pltpu.VMEM((m, LANES), F32)],
        compiler_params=pltpu.CompilerParams(dimension_semantics=("arbitrary", "arbitrary"),
                                             vmem_limit_bytes=VMEM_LIMIT),
        name="sb_attn_sample",
    )(q, kn, vn, cache_k, cache_v)


def kernel(x_prompt, x_sample, cache_k, cache_v, state_pool, p_prompt, p_sample,
           ln_g, ln_b, ffn_w1, ffn_w3, ffn_w2, pool_w, pool_scale,
           sb_w_in, sb_w_out, ple_w_gate, ple_b_gate, ple_w_proj):
    depth = ln_g.shape[0]
    bp, sp, d = x_prompt.shape
    bs, ss, _ = x_sample.shape
    n_sb, _, past, heads, hd = cache_k.shape
    assert heads == N_HEADS and hd * heads == d and LANES == 2 * hd
    alpha = (2.0 * depth) ** 0.25
    sb_scale = 1.0 / math.sqrt(hd)

    w1 = ffn_w1.astype(BF16)
    w3 = ffn_w3.astype(BF16)
    w2 = ffn_w2.astype(BF16)
    pw = pool_w.astype(BF16)
    w_in = sb_w_in.astype(BF16)
    w_out = sb_w_out.astype(BF16)
    wg = ple_w_gate.astype(BF16)
    wp = ple_w_proj.astype(BF16)

    xp = x_prompt.reshape(bp * sp, d)
    xs = x_sample.reshape(bs * ss, d)
    pp = p_prompt.reshape(depth, bp * sp, -1)
    ps = p_sample.reshape(depth, bs * ss, -1)
    ck = cache_k.reshape(n_sb, bs, past, d)
    cv = cache_v.reshape(n_sb, bs, past, d)

    kp_all = vp_all = ks_all = vs_all = None
    pool_p, pool_s = [], []

    for i in range(depth):
        j = i // 2
        g, b = ln_g[i], ln_b[i]
        ffn0 = (w1[i, 0], w3[i, 0], w2[i, 0])
        ffn1 = (w1[i, 1], w3[i, 1], w2[i, 1])
        tail = (g, b) + ffn1 + (wg[i], ple_b_gate[i][None, :], wp[i])
        if i % 2 == 0:
            xp1 = _pre_call(xp, *ffn0, g, b, alpha)
            xs1 = _pre_call(xs, *ffn0, g, b, alpha)
            scale = pool_scale[j][None, :]
            xp, npp = _post_pool_call(xp1, sp, pp, i, pw[j], scale, *tail, alpha)
            prev_s = jnp.pad(state_pool[j], ((0, 0), (1, 0), (0, 0)))
            ms, nps = _pool_call(xs1.reshape(bs, ss, d), prev_s, pw[j], scale, past)
            xs = _post_call(xs1, ms.reshape(bs * ss, d), ps, i, None, *tail, alpha)
            pool_p.append(npp)
            pool_s.append(nps)
        else:
            xp1, qp, kp_all, vp_all = _pre_qkv_call(xp, *ffn0, g, b, w_in[j], kp_all, vp_all, j,
                                                    n_sb, alpha, sb_scale)
            xs1, qs, ks_all, vs_all = _pre_qkv_call(xs, *ffn0, g, b, w_in[j], ks_all, vs_all, j,
                                                    n_sb, alpha, sb_scale)
            op = _attn_prompt_call(qp.reshape(bp, sp, d), kp_all.reshape(n_sb, bp, sp, d),
                                   vp_all.reshape(n_sb, bp, sp, d), j)
            osm = _attn_sample_call(qs.reshape(bs, ss, d), ks_all[j].reshape(bs, ss, d),
                                    vs_all[j].reshape(bs, ss, d), ck, cv, j)
            xp = _post_call(xp1, op.reshape(bp * sp, d), pp, i, w_out[j], *tail, alpha)
            xs = _post_call(xs1, osm.reshape(bs * ss, d), ps, i, w_out[j], *tail, alpha)

    return (xp.reshape(bp, sp, d), xs.reshape(bs, ss, d),
            kp_all.reshape(n_sb, bp, sp, heads, hd), vp_all.reshape(n_sb, bp, sp, heads, hd),
            ks_all.reshape(n_sb, bs, ss, heads, hd), vs_all.reshape(n_sb, bs, ss, heads, hd),
            jnp.stack(pool_p), jnp.stack(pool_s))
```

```python
import functools
import math

import jax
import jax.numpy as jnp
from jax import lax
from jax.experimental import pallas as pl
from jax.experimental.pallas import tpu as pltpu

N_HEADS = 16
POOL_WINDOWS = (2, 4, 8, 16)
POOL_MAX = max(POOL_WINDOWS)
LN_EPS = 1e-5

LANES = 128
VMEM_LIMIT = 52 * 1024 * 1024
ROW_TILE = 512
FF_CHUNK = 512
ATT_TILE = 256
MASK_BIAS = -1e30
SKIP_MASS = 104.0
SOFTPLUS_CLAMP = 80.0

BF16 = jnp.bfloat16
F32 = jnp.float32


def _dot(a, b):
    return jnp.dot(a, b, preferred_element_type=F32)


def _dot_nt(a, b):
    return lax.dot_general(a, b, (((1,), (1,)), ((), ())), preferred_element_type=F32)


def _ln(x, g, b):
    mu = jnp.mean(x, axis=-1, keepdims=True)
    xc = x - mu
    var = jnp.mean(xc * xc, axis=-1, keepdims=True)
    return xc * lax.rsqrt(var + LN_EPS) * g + b


def _half_ffn(x, w1_ref, w3_ref, w2_ref, g, b, alpha):
    d_ff = w1_ref.shape[1]
    xb = x.astype(BF16)
    acc = None
    for c0 in range(0, d_ff, FF_CHUNK):
        c1 = min(c0 + FF_CHUNK, d_ff)
        a = _dot(xb, w1_ref[:, c0:c1])
        u = _dot(xb, w3_ref[:, c0:c1])
        h = (a * jax.nn.sigmoid(a) * u).astype(BF16)
        part = _dot(h, w2_ref[c0:c1, :])
        acc = part if acc is None else acc + part
    return _ln(alpha * x + 0.5 * acc, g, b)


def _post_chain(x1, m, p, g_ref, b_ref, w1_ref, w3_ref, w2_ref, wg_ref, bg_ref, wp_ref, alpha):
    x2 = _ln(alpha * x1 + m, g_ref[1:2, :], b_ref[1:2, :])
    x3 = _half_ffn(x2, w1_ref, w3_ref, w2_ref, g_ref[2:3, :], b_ref[2:3, :], alpha)
    gate = jax.nn.sigmoid(_dot(x3.astype(BF16), wg_ref[...]) + bg_ref[...])
    e = _dot(p.astype(BF16), wp_ref[...])
    return _ln(alpha * x3 + gate * e, g_ref[3:4, :], b_ref[3:4, :])


def _pool_mix(x, xe_ref, pos, pw_ref, scale_ref):
    tm = x.shape[0]
    gw = x.shape[1] // len(POOL_WINDOWS)
    parts = []
    for g, w in enumerate(POOL_WINDOWS):
        c0, c1 = g * gw, (g + 1) * gw
        xg = x[:, c0:c1]
        s = xg
        for k in range(1, w):
            s = s + xe_ref[POOL_MAX - k:POOL_MAX - k + tm, c0:c1]
        cnt = jnp.minimum(w, pos + 1).astype(F32)
        y = (s / cnt - xg).astype(BF16)
        parts.append(_dot(y, pw_ref[g]) * scale_ref[:, c0:c1])
    return jnp.concatenate(parts, axis=1)


def _pre_kernel(x_ref, w1_ref, w3_ref, w2_ref, g_ref, b_ref, o_ref, *, alpha):
    o_ref[...] = _half_ffn(x_ref[...], w1_ref, w3_ref, w2_ref, g_ref[0:1, :], b_ref[0:1, :], alpha)


def _pre_qkv_kernel(x_ref, w1_ref, w3_ref, w2_ref, g_ref, b_ref, win_ref, *refs, alpha, sb_scale):
    x1_ref, q_ref, k_ref, v_ref = refs[-4:]
    d = x_ref.shape[1]
    x1 = _half_ffn(x_ref[...], w1_ref, w3_ref, w2_ref, g_ref[0:1, :], b_ref[0:1, :], alpha)
    x1_ref[...] = x1
    xb = x1.astype(BF16)
    q_ref[...] = _dot(xb, win_ref[:, 0:d]) * sb_scale
    k_ref[...] = _dot(xb, win_ref[:, d:2 * d])
    v_ref[...] = _dot(xb, win_ref[:, 2 * d:3 * d])


def _post_kernel(*refs, alpha, has_wout):
    if has_wout:
        x1_ref, m_ref, p_ref, wout_ref = refs[:4]
        m = _dot(m_ref[...].astype(BF16), wout_ref[...])
    else:
        x1_ref, m_ref, p_ref = refs[:3]
        m = m_ref[...]
    o_ref = refs[-1]
    o_ref[...] = _post_chain(x1_ref[...], m, p_ref[...], *refs[-9:-1], alpha)


def _post_pool_kernel(x1_ref, p_ref, pw_ref, scale_ref, g_ref, b_ref, w1_ref, w3_ref, w2_ref,
                      wg_ref, bg_ref, wp_ref, o_ref, np_ref, xe_ref, *, alpha, tiles_per_seq):
    tm = x1_ref.shape[0]
    t = pl.program_id(0) % tiles_per_seq

    @pl.when(t == 0)
    def _():
        xe_ref[0:POOL_MAX, :] = jnp.zeros((POOL_MAX, xe_ref.shape[1]), F32)

    @pl.when(t > 0)
    def _():
        xe_ref[0:POOL_MAX, :] = xe_ref[tm:tm + POOL_MAX, :]

    x1 = x1_ref[...]
    xe_ref[POOL_MAX:POOL_MAX + tm, :] = x1
    pos = t * tm + lax.broadcasted_iota(jnp.int32, (tm, 1), 0)
    m = _pool_mix(x1, xe_ref, pos, pw_ref, scale_ref)
    np_ref[0] = x1[tm - (POOL_MAX - 1):tm, :]
    o_ref[...] = _post_chain(x1, m, p_ref[...], g_ref, b_ref, w1_ref, w3_ref, w2_ref,
                             wg_ref, bg_ref, wp_ref, alpha)


def _const_spec(shape):
    nd = len(shape)
    return pl.BlockSpec(shape, lambda *_: (0,) * nd, pipeline_mode=pl.Buffered(1))


def _row_spec(tm, width):
    return pl.BlockSpec((tm, width), lambda i: (i, 0))


def _row_params():
    return pltpu.CompilerParams(dimension_semantics=("arbitrary",), vmem_limit_bytes=VMEM_LIMIT)


def _pre_call(x, w1, w3, w2, g, b, alpha):
    rows, d = x.shape
    tm = min(ROW_TILE, rows)
    return pl.pallas_call(
        functools.partial(_pre_kernel, alpha=alpha),
        grid=(rows // tm,),
        in_specs=[_row_spec(tm, d), _const_spec(w1.shape), _const_spec(w3.shape),
                  _const_spec(w2.shape), _const_spec(g.shape), _const_spec(b.shape)],
        out_specs=_row_spec(tm, d),
        out_shape=jax.ShapeDtypeStruct((rows, d), F32),
        compiler_params=_row_params(),
        name="pre_ffn",
    )(x, w1, w3, w2, g, b)


def _pre_qkv_call(x, w1, w3, w2, g, b, w_in, k_all, v_all, j, n_sb, alpha, sb_scale):
    rows, d = x.shape
    tm = min(ROW_TILE, rows)
    kv_spec = pl.BlockSpec((None, tm, d), lambda i: (j, i, 0))
    args = [x, w1, w3, w2, g, b, w_in]
    specs = [_row_spec(tm, d)] + [_const_spec(a.shape) for a in args[1:]]
    aliases = {}
    if k_all is not None:
        aliases = {len(args): 2, len(args) + 1: 3}
        args += [k_all, v_all]
        specs += [pl.BlockSpec(memory_space=pl.ANY)] * 2
    return pl.pallas_call(
        functools.partial(_pre_qkv_kernel, alpha=alpha, sb_scale=sb_scale),
        grid=(rows // tm,),
        in_specs=specs,
        out_specs=[_row_spec(tm, d), _row_spec(tm, d), kv_spec, kv_spec],
        out_shape=[jax.ShapeDtypeStruct((rows, d), F32), jax.ShapeDtypeStruct((rows, d), F32),
                   jax.ShapeDtypeStruct((n_sb, rows, d), F32),
                   jax.ShapeDtypeStruct((n_sb, rows, d), F32)],
        input_output_aliases=aliases,
        compiler_params=_row_params(),
        name="pre_ffn_qkv",
    )(*args)


def _post_call(x1, m, p_all, i, w_out, g, b, w1, w3, w2, wg, bg, wp, alpha):
    rows, d = x1.shape
    tm = min(ROW_TILE, rows)
    has_wout = w_out is not None
    args = [x1, m, p_all] + ([w_out] if has_wout else []) + [g, b, w1, w3, w2, wg, bg, wp]
    specs = [_row_spec(tm, d), _row_spec(tm, d),
             pl.BlockSpec((None, tm, p_all.shape[2]), lambda r: (i, r, 0))]
    specs += [_const_spec(a.shape) for a in args[3:]]
    return pl.pallas_call(
        functools.partial(_post_kernel, alpha=alpha, has_wout=has_wout),
        grid=(rows // tm,),
        in_specs=specs,
        out_specs=_row_spec(tm, d),
        out_shape=jax.ShapeDtypeStruct((rows, d), F32),
        compiler_params=_row_params(),
        name="post_mixer",
    )(*args)


def _post_pool_call(x1, seq, p_all, i, pw, scale, g, b, w1, w3, w2, wg, bg, wp, alpha):
    rows, d = x1.shape
    tm = min(ROW_TILE, seq)
    tiles_per_seq = seq // tm
    consts = [pw, scale, g, b, w1, w3, w2, wg, bg, wp]
    specs = [_row_spec(tm, d), pl.BlockSpec((None, tm, p_all.shape[2]), lambda r: (i, r, 0))]
    specs += [_const_spec(a.shape) for a in consts]
    return pl.pallas_call(
        functools.partial(_post_pool_kernel, alpha=alpha, tiles_per_seq=tiles_per_seq),
        grid=(rows // tm,),
        in_specs=specs,
        out_specs=[_row_spec(tm, d),
                   pl.BlockSpec((1, POOL_MAX - 1, d), lambda r: (r // tiles_per_seq, 0, 0))],
        out_shape=[jax.ShapeDtypeStruct((rows, d), F32),
                   jax.ShapeDtypeStruct((rows // seq, POOL_MAX - 1, d), F32)],
        scratch_shapes=[pltpu.VMEM((tm + POOL_MAX, d), F32)],
        compiler_params=_row_params(),
        name="post_pool_mixer",
    )(x1, p_all, *consts)


def _pool_kernel(x_ref, prev_ref, pw_ref, scale_ref, m_ref, np_ref, xe_ref, *, pos0):
    tm = x_ref.shape[1]
    x = x_ref[0]
    xe_ref[0:POOL_MAX, :] = prev_ref[0]
    xe_ref[POOL_MAX:POOL_MAX + tm, :] = x
    pos = pos0 + lax.broadcasted_iota(jnp.int32, (tm, 1), 0)
    m_ref[0] = _pool_mix(x, xe_ref, pos, pw_ref, scale_ref)
    np_ref[0] = x[tm - (POOL_MAX - 1):tm, :]


def _pool_call(x, prev16, pw, scale, pos0):
    bsz, s, d = x.shape
    return pl.pallas_call(
        functools.partial(_pool_kernel, pos0=pos0),
        grid=(bsz,),
        in_specs=[pl.BlockSpec((1, s, d), lambda b: (b, 0, 0)),
                  pl.BlockSpec((1, POOL_MAX, d), lambda b: (b, 0, 0)),
                  pl.BlockSpec(pw.shape, lambda b: (0, 0, 0)),
                  pl.BlockSpec(scale.shape, lambda b: (0, 0))],
        out_specs=[pl.BlockSpec((1, s, d), lambda b: (b, 0, 0)),
                   pl.BlockSpec((1, POOL_MAX - 1, d), lambda b: (b, 0, 0))],
        out_shape=[jax.ShapeDtypeStruct((bsz, s, d), F32),
                   jax.ShapeDtypeStruct((bsz, POOL_MAX - 1, d), F32)],
        scratch_shapes=[pltpu.VMEM((s + POOL_MAX, d), F32)],
        compiler_params=_row_params(),
        name="pool_mixer",
    )(x, prev16, pw, scale)


def _tri_strict(t):
    r = lax.broadcasted_iota(jnp.int32, (t, t), 0)
    c = lax.broadcasted_iota(jnp.int32, (t, t), 1)
    return jnp.where(r > c, 1.0, 0.0)


def _softplus(z):
    return jnp.maximum(z, jnp.log(1.0 + jnp.exp(jnp.minimum(z, SOFTPLUS_CLAMP))))


def _sb_block(z, v, tri, carry, mask):
    sp = _softplus(z)
    ls = z - sp
    if mask is not None:
        sp = jnp.where(mask, sp, 0.0)
    cs = _dot(sp, tri)
    w = jnp.exp(ls - cs - carry)
    if mask is not None:
        w = jnp.where(mask, w, 0.0)
    return _dot(w, v), carry + cs[:, 0:1] + sp[:, 0:1]


def _head_split(q):
    lane = lax.broadcasted_iota(jnp.int32, (1, LANES), 1)
    zero = jnp.zeros_like(q)
    return [jnp.where(lane < LANES // 2, q, zero), jnp.where(lane >= LANES // 2, q, zero)]


def _head_merge(o0, o1):
    lane = lax.broadcasted_iota(jnp.int32, (1, LANES), 1)
    return jnp.where(lane < LANES // 2, o0, o1)


def _attn_prompt_kernel(q_ref, k_ref, v_ref, o_ref, tri_ref, bias_ref,
                        ls0_ref, ls1_ref, sp0_ref, sp1_ref, w0_ref, w1_ref,
                        carry_ref, tot_ref, *, seq, tile):
    nblk = seq // tile
    assert nblk >= 2
    n_items = 2 * nblk - 1
    sink_blk = nblk
    ls_refs, sp_refs, w_refs = (ls0_ref, ls1_ref), (sp0_ref, sp1_ref), (w0_ref, w1_ref)

    @pl.when((pl.program_id(0) == 0) & (pl.program_id(1) == 0))
    def _():
        r = lax.broadcasted_iota(jnp.int32, (tile, tile), 0)
        c = lax.broadcasted_iota(jnp.int32, (tile, tile), 1)
        tri_ref[...] = _tri_strict(tile)
        bias_ref[...] = jnp.where(c < r, 0.0, MASK_BIAS)

    def rows(i):
        return pl.ds(pl.multiple_of(i * tile, tile), tile)

    def item(n, diag):
        qi = jnp.minimum((n + 1) // 2, nblk - 1)
        return qi, (qi if diag else qi - 1)

    def stage_a(n, slot, diag):
        qi, kj = item(n, diag)
        k = k_ref[0, rows(kj), :]
        for h, qh in enumerate(_head_split(q_ref[0, rows(qi), :])):
            z = _dot_nt(qh, k)
            if diag:
                z = z + bias_ref[...]
            sp = _softplus(z)
            sp_refs[slot][h] = sp
            ls_refs[slot][h] = z - sp

    def stage_b(n, slot, diag):
        qi, _ = item(n, diag)
        tot_idx = jnp.where(n < n_items, qi, sink_blk)
        tri = tri_ref[...]
        for h in range(2):
            sp = sp_refs[slot][h]
            cs = _dot(sp, tri)
            e = ls_refs[slot][h] - cs
            mass = cs[:, 0:1] + sp[:, 0:1]
            if not diag:
                c_in = carry_ref[h][:, 0:1]
                e = e - c_in
                mass = mass + c_in
            w_refs[slot][h] = jnp.exp(e)
            mass = jnp.broadcast_to(mass, (tile, LANES))
            if diag:
                carry_ref[h] = mass
            tot_ref[tot_idx, h] = mass

    def stage_c(n, slot, diag):
        qi, kj = item(n, diag)
        v = v_ref[0, rows(kj), :]
        pv = _head_merge(_dot(w_refs[slot][0], v), _dot(w_refs[slot][1], v))
        if diag:
            o_ref[0, rows(qi), :] = pv
        else:
            o_ref[0, rows(qi), :] += pv

    stage_a(0, 0, True)
    stage_b(0, 0, True)
    stage_a(1, 1, True)
    stage_c(0, 0, True)
    stage_b(1, 1, True)
    stage_a(2, 0, False)

    def pair_body(m, _):
        n = 2 * m + 3
        stage_c(n - 2, 1, True)
        stage_b(n - 1, 0, False)
        stage_a(n, 1, True)
        stage_c(n - 1, 0, False)
        stage_b(n, 1, True)
        stage_a(n + 1, 0, False)
        return 0

    lax.fori_loop(0, nblk - 1, pair_body, 0)

    def tail_body(qi, _):
        @pl.when(jnp.min(tot_ref[qi]) < SKIP_MASS)
        def _():
            qh = _head_split(q_ref[0, rows(qi), :])
            tri = tri_ref[...]

            def cond(state):
                kj, c0, c1 = state
                return jnp.logical_and(kj >= 0, jnp.minimum(jnp.min(c0), jnp.min(c1)) < SKIP_MASS)

            def body(state):
                kj, c0, c1 = state
                k = k_ref[0, rows(kj), :]
                v = v_ref[0, rows(kj), :]
                pv0, c0 = _sb_block(_dot_nt(qh[0], k), v, tri, c0, None)
                pv1, c1 = _sb_block(_dot_nt(qh[1], k), v, tri, c1, None)
                o_ref[0, rows(qi), :] += _head_merge(pv0, pv1)
                return (kj - 1, c0, c1)

            lax.while_loop(cond, body, (qi - 2, tot_ref[qi, 0][:, 0:1], tot_ref[qi, 1][:, 0:1]))
        return 0

    if nblk > 2:
        @pl.when(jnp.min(tot_ref[2:nblk]) < SKIP_MASS)
        def _():
            lax.fori_loop(2, nblk, tail_body, 0)


def _attn_prompt_call(q, k_all, v_all, j):
    bsz, s, d = q.shape
    tile = min(ATT_TILE, s)
    nblk = s // tile
    spec = pl.BlockSpec((1, s, LANES), lambda b, h: (b, 0, h))
    kv_spec = pl.BlockSpec((None, 1, s, LANES), lambda b, h: (j, b, 0, h))
    return pl.pallas_call(
        functools.partial(_attn_prompt_kernel, seq=s, tile=tile),
        grid=(bsz, d // LANES),
        in_specs=[spec, kv_spec, kv_spec],
        out_specs=spec,
        out_shape=jax.ShapeDtypeStruct((bsz, s, d), F32),
        scratch_shapes=[pltpu.VMEM((tile, tile), F32),
                        pltpu.VMEM((tile, tile), F32)]
                       + [pltpu.VMEM((2, tile, tile), F32)] * 6
                       + [pltpu.VMEM((2, tile, LANES), F32),
                          pltpu.VMEM((nblk + 1, 2, tile, LANES), F32)],
        compiler_params=pltpu.CompilerParams(dimension_semantics=("arbitrary", "arbitrary"),
                                             vmem_limit_bytes=VMEM_LIMIT),
        name="sb_attn_prompt",
    )(q, k_all, v_all)


def _attn_sample_kernel(*refs, n_new, tile, blk, resume):
    if resume:
        q_ref, acc_in_ref, mass_in_ref, ck_ref, cv_ref, o_ref, acc_ref, mass_ref, qbd_ref = refs
    else:
        q_ref, kn_ref, vn_ref, ck_ref, cv_ref, o_ref, acc_ref, mass_ref, qbd_ref = refs
    t = pl.program_id(1)
    d = q_ref.shape[2]
    hd = d // N_HEADS
    m = N_HEADS * n_new
    tri = _tri_strict(tile)

    @pl.when(t == 0)
    def _():
        q = q_ref[0]
        col = lax.broadcasted_iota(jnp.int32, (n_new, d), 1)
        for h in range(N_HEADS):
            qbd_ref[h * n_new:(h + 1) * n_new, :] = jnp.where(col // hd == h, q, jnp.zeros_like(q))
        if resume:
            acc_ref[0] = acc_in_ref[0]
            mass_ref[0] = mass_in_ref[0]
        else:
            t_idx = lax.broadcasted_iota(jnp.int32, (m, tile), 0) % n_new
            s_idx = lax.broadcasted_iota(jnp.int32, (m, tile), 1)
            z = _dot_nt(qbd_ref[...], kn_ref[0])
            pv, cr = _sb_block(z, vn_ref[0], tri, jnp.zeros((m, 1), F32), s_idx < t_idx)
            acc_ref[0] = pv
            mass_ref[0] = jnp.broadcast_to(cr, (m, LANES))

    qbd = qbd_ref[...]
    for i in reversed(range(blk // tile)):
        kb = ck_ref[0, i * tile:(i + 1) * tile, :]
        vb = cv_ref[0, i * tile:(i + 1) * tile, :]
        pv, cr = _sb_block(_dot_nt(qbd, kb), vb, tri, mass_ref[0][:, 0:1], None)
        acc_ref[0] += pv
        mass_ref[0] = jnp.broadcast_to(cr, (m, LANES))

    @pl.when(t == pl.num_programs(1) - 1)
    def _():
        col = lax.broadcasted_iota(jnp.int32, (n_new, d), 1)
        o = jnp.zeros((n_new, d), F32)
        for h in range(N_HEADS):
            o = o + jnp.where(col // hd == h, acc_ref[0, h * n_new:(h + 1) * n_new, :], 0.0)
        o_ref[0] = o


def _attn_sample_call(q, first, caches, cache_spec, nkb, blk, resume):
    bsz, n_new, d = q.shape
    m = N_HEADS * n_new
    q_spec = pl.BlockSpec((1, n_new, d), lambda b, t: (b, 0, 0))
    acc_spec = pl.BlockSpec((1, m, d), lambda b, t: (b, 0, 0))
    mass_spec = pl.BlockSpec((1, m, LANES), lambda b, t: (b, 0, 0))
    if resume:
        first_specs = [acc_spec, mass_spec]
    else:
        first_specs = [pl.BlockSpec((1, ATT_TILE, d), lambda b, t: (b, 0, 0))] * 2
    return pl.pallas_call(
        functools.partial(_attn_sample_kernel, n_new=n_new, tile=ATT_TILE, blk=blk, resume=resume),
        grid=(bsz, nkb),
        in_specs=[q_spec] + first_specs + [cache_spec, cache_spec],
        out_specs=[q_spec, acc_spec, mass_spec],
        out_shape=[jax.ShapeDtypeStruct((bsz, n_new, d), F32),
                   jax.ShapeDtypeStruct((bsz, m, d), F32),
                   jax.ShapeDtypeStruct((bsz, m, LANES), F32)],
        scratch_shapes=[pltpu.VMEM((m, d), F32)],
        compiler_params=pltpu.CompilerParams(dimension_semantics=("arbitrary", "arbitrary"),
                                             vmem_limit_bytes=VMEM_LIMIT),
        name="sb_attn_sample",
    )(q, *first, *caches)


def _attn_sample(q, k_new, v_new, cache_k, cache_v, j):
    bsz, n_new, d = q.shape
    n_sb, _, past = cache_k.shape[:3]
    tile = ATT_TILE
    assert past % tile == 0 and n_new <= tile
    pad = ((0, 0), (0, tile - n_new), (0, 0))
    tail_k = cache_k[j, :, past - tile:].reshape(bsz, tile, d)
    tail_v = cache_v[j, :, past - tile:].reshape(bsz, tile, d)
    tail_spec = pl.BlockSpec((1, tile, d), lambda b, t: (b, 0, 0))
    o, acc, mass = _attn_sample_call(q, (jnp.pad(k_new, pad), jnp.pad(v_new, pad)),
                                     (tail_k, tail_v), tail_spec, 1, tile, False)
    n_rest = past // tile - 1
    if n_rest == 0:
        return o

    def older_keys(q, acc, mass, cache_k, cache_v):
        ck = cache_k.reshape(n_sb, bsz, past, d)
        cv = cache_v.reshape(n_sb, bsz, past, d)
        rest_spec = pl.BlockSpec((None, 1, tile, d), lambda b, t: (j, b, n_rest - 1 - t, 0))
        return _attn_sample_call(q, (acc, mass), (ck, cv), rest_spec, n_rest, tile, True)[0]

    return lax.cond(jnp.min(mass) < SKIP_MASS,
                    lambda o, *rest: older_keys(*rest), lambda o, *rest: o,
                    o, q, acc, mass, cache_k, cache_v)


def kernel(x_prompt, x_sample, cache_k, cache_v, state_pool, p_prompt, p_sample,
           ln_g, ln_b, ffn_w1, ffn_w3, ffn_w2, pool_w, pool_scale,
           sb_w_in, sb_w_out, ple_w_gate, ple_b_gate, ple_w_proj):
    depth = ln_g.shape[0]
    bp, sp, d = x_prompt.shape
    bs, ss, _ = x_sample.shape
    n_sb, _, past, heads, hd = cache_k.shape
    assert heads == N_HEADS and hd * heads == d and LANES == 2 * hd
    alpha = (2.0 * depth) ** 0.25
    sb_scale = 1.0 / math.sqrt(hd)

    w1 = ffn_w1.astype(BF16)
    w3 = ffn_w3.astype(BF16)
    w2 = ffn_w2.astype(BF16)
    pw = pool_w.astype(BF16)
    w_in = sb_w_in.astype(BF16)
    w_out = sb_w_out.astype(BF16)
    wg = ple_w_gate.astype(BF16)
    wp = ple_w_proj.astype(BF16)

    xp = x_prompt.reshape(bp * sp, d)
    xs = x_sample.reshape(bs * ss, d)
    pp = p_prompt.reshape(depth, bp * sp, -1)
    ps = p_sample.reshape(depth, bs * ss, -1)

    kp_all = vp_all = ks_all = vs_all = None
    pool_p, pool_s = [], []

    for i in range(depth):
        j = i // 2
        g, b = ln_g[i], ln_b[i]
        ffn0 = (w1[i, 0], w3[i, 0], w2[i, 0])
        ffn1 = (w1[i, 1], w3[i, 1], w2[i, 1])
        tail = (g, b) + ffn1 + (wg[i], ple_b_gate[i][None, :], wp[i])
        if i % 2 == 0:
            xp1 = _pre_call(xp, *ffn0, g, b, alpha)
            xs1 = _pre_call(xs, *ffn0, g, b, alpha)
            scale = pool_scale[j][None, :]
            xp, npp = _post_pool_call(xp1, sp, pp, i, pw[j], scale, *tail, alpha)
            prev_s = jnp.pad(state_pool[j], ((0, 0), (1, 0), (0, 0)))
            ms, nps = _pool_call(xs1.reshape(bs, ss, d), prev_s, pw[j], scale, past)
            xs = _post_call(xs1, ms.reshape(bs * ss, d), ps, i, None, *tail, alpha)
            pool_p.append(npp)
            pool_s.append(nps)
        else:
            xp1, qp, kp_all, vp_all = _pre_qkv_call(xp, *ffn0, g, b, w_in[j], kp_all, vp_all, j,
                                                    n_sb, alpha, sb_scale)
            xs1, qs, ks_all, vs_all = _pre_qkv_call(xs, *ffn0, g, b, w_in[j], ks_all, vs_all, j,
                                                    n_sb, alpha, sb_scale)
            op = _attn_prompt_call(qp.reshape(bp, sp, d), kp_all.reshape(n_sb, bp, sp, d),
                                   vp_all.reshape(n_sb, bp, sp, d), j)
            osm = _attn_sample(qs.reshape(bs, ss, d), ks_all[j].reshape(bs, ss, d),
                               vs_all[j].reshape(bs, ss, d), cache_k, cache_v, j)
            xp = _post_call(xp1, op.reshape(bp * sp, d), pp, i, w_out[j], *tail, alpha)
            xs = _post_call(xs1, osm.reshape(bs * ss, d), ps, i, w_out[j], *tail, alpha)

    return (xp.reshape(bp, sp, d), xs.reshape(bs, ss, d),
            kp_all.reshape(n_sb, bp, sp, heads, hd), vp_all.reshape(n_sb, bp, sp, heads, hd),
            ks_all.reshape(n_sb, bs, ss, heads, hd), vs_all.reshape(n_sb, bs, ss, heads, hd),
            jnp.stack(pool_p), jnp.stack(pool_s))
```

```python
import functools
import math

import jax
import jax.numpy as jnp
from jax import lax
from jax.experimental import pallas as pl
from jax.experimental.pallas import tpu as pltpu

N_HEADS = 16
POOL_WINDOWS = (2, 4, 8, 16)
POOL_MAX = max(POOL_WINDOWS)
LN_EPS = 1e-5

LANES = 128
VMEM_LIMIT = 52 * 1024 * 1024
ROW_TILE = 512
FF_CHUNK = 256
ATT_TILE = 256
MASK_BIAS = -1e30
SKIP_MASS = 104.0
SOFTPLUS_CLAMP = 80.0

BF16 = jnp.bfloat16
F32 = jnp.float32


def _dot(a, b):
    return jnp.dot(a, b, preferred_element_type=F32)


def _dot_nt(a, b):
    return lax.dot_general(a, b, (((1,), (1,)), ((), ())), preferred_element_type=F32)


def _ln(x, g, b):
    mu = jnp.mean(x, axis=-1, keepdims=True)
    xc = x - mu
    var = jnp.mean(xc * xc, axis=-1, keepdims=True)
    return xc * lax.rsqrt(var + LN_EPS) * g + b


def _half_ffn(x, w1_ref, w3_ref, w2_ref, g, b, alpha):
    d_ff = w1_ref.shape[1]
    xb = x.astype(BF16)
    acc = None
    for c0 in range(0, d_ff, FF_CHUNK):
        c1 = min(c0 + FF_CHUNK, d_ff)
        a = _dot(xb, w1_ref[:, c0:c1])
        u = _dot(xb, w3_ref[:, c0:c1])
        h = (a * jax.nn.sigmoid(a) * u).astype(BF16)
        part = _dot(h, w2_ref[c0:c1, :])
        acc = part if acc is None else acc + part
    return _ln(alpha * x + 0.5 * acc, g, b)


def _post_chain(x1, m, p, g_ref, b_ref, w1_ref, w3_ref, w2_ref, wg_ref, bg_ref, wp_ref, alpha):
    x2 = _ln(alpha * x1 + m, g_ref[1:2, :], b_ref[1:2, :])
    x3 = _half_ffn(x2, w1_ref, w3_ref, w2_ref, g_ref[2:3, :], b_ref[2:3, :], alpha)
    gate = jax.nn.sigmoid(_dot(x3.astype(BF16), wg_ref[...]) + bg_ref[...])
    e = _dot(p.astype(BF16), wp_ref[...])
    return _ln(alpha * x3 + gate * e, g_ref[3:4, :], b_ref[3:4, :])


def _pool_mix(x, xe_ref, pos, pw_ref, scale_ref):
    gw = x.shape[1] // len(POOL_WINDOWS)
    parts = [_pool_group(x[:, g * gw:(g + 1) * gw], xe_ref, pos, pw_ref, scale_ref, g, 0)
             for g in range(len(POOL_WINDOWS))]
    return jnp.concatenate(parts, axis=1)


def _pool_group(xg, xe_ref, pos, pw_ref, scale_ref, g, r0):
    n, gw = xg.shape
    w = POOL_WINDOWS[g]
    c0, c1 = g * gw, (g + 1) * gw
    s = xg
    for k in range(1, w):
        s = s + xe_ref[POOL_MAX - k + r0:POOL_MAX - k + r0 + n, c0:c1]
    cnt = jnp.minimum(w, pos + 1).astype(F32)
    y = (s / cnt - xg).astype(BF16)
    return _dot(y, pw_ref[g]) * scale_ref[:, c0:c1]


def _pre_kernel(x_ref, w1_ref, w3_ref, w2_ref, g_ref, b_ref, o_ref, *, alpha):
    o_ref[...] = _half_ffn(x_ref[...], w1_ref, w3_ref, w2_ref, g_ref[0:1, :], b_ref[0:1, :], alpha)


def _pre_qkv_kernel(x_ref, w1_ref, w3_ref, w2_ref, g_ref, b_ref, win_ref, *refs, alpha, sb_scale):
    x1_ref, q_ref, k_ref, v_ref = refs[-4:]
    d = x_ref.shape[1]
    x1 = _half_ffn(x_ref[...], w1_ref, w3_ref, w2_ref, g_ref[0:1, :], b_ref[0:1, :], alpha)
    x1_ref[...] = x1
    xb = x1.astype(BF16)
    q_ref[...] = _dot(xb, win_ref[:, 0:d]) * sb_scale
    k_ref[...] = _dot(xb, win_ref[:, d:2 * d])
    v_ref[...] = _dot(xb, win_ref[:, 2 * d:3 * d])


def _post_kernel(*refs, alpha, has_wout):
    if has_wout:
        x1_ref, m_ref, p_ref, wout_ref = refs[:4]
        m = _dot(m_ref[...].astype(BF16), wout_ref[...])
    else:
        x1_ref, m_ref, p_ref = refs[:3]
        m = m_ref[...]
    o_ref = refs[-1]
    o_ref[...] = _post_chain(x1_ref[...], m, p_ref[...], *refs[-9:-1], alpha)


def _post_pool_kernel(x1_ref, p_ref, pw_ref, scale_ref, g_ref, b_ref, w1_ref, w3_ref, w2_ref,
                      wg_ref, bg_ref, wp_ref, o_ref, np_ref, xe_ref, *, alpha, tiles_per_seq):
    tm = x1_ref.shape[0]
    t = pl.program_id(0) % tiles_per_seq

    @pl.when(t == 0)
    def _():
        xe_ref[0:POOL_MAX, :] = jnp.zeros((POOL_MAX, xe_ref.shape[1]), F32)

    @pl.when(t > 0)
    def _():
        xe_ref[0:POOL_MAX, :] = xe_ref[tm:tm + POOL_MAX, :]

    x1 = x1_ref[...]
    xe_ref[POOL_MAX:POOL_MAX + tm, :] = x1
    pos = t * tm + lax.broadcasted_iota(jnp.int32, (tm, 1), 0)
    m = _pool_mix(x1, xe_ref, pos, pw_ref, scale_ref)
    np_ref[0] = x1[tm - (POOL_MAX - 1):tm, :]
    o_ref[...] = _post_chain(x1, m, p_ref[...], g_ref, b_ref, w1_ref, w3_ref, w2_ref,
                             wg_ref, bg_ref, wp_ref, alpha)


def _slab_specs(slabs):
    specs = []
    for arr, lead in slabs:
        shape = (None,) * len(lead) + tuple(arr.shape[len(lead):])
        idx = tuple(lead) + (0,) * (arr.ndim - len(lead))
        specs.append(pl.BlockSpec(shape, lambda *_, idx=idx: idx, pipeline_mode=pl.Buffered(1)))
    return specs


def _arrays(slabs):
    return [arr for arr, _ in slabs]


def _row_spec(tm, width):
    return pl.BlockSpec((tm, width), lambda i: (i, 0))


def _row_params():
    return pltpu.CompilerParams(dimension_semantics=("arbitrary",), vmem_limit_bytes=VMEM_LIMIT)


def _pre_call(x, slabs, alpha):
    rows, d = x.shape
    tm = min(ROW_TILE, rows)
    return pl.pallas_call(
        functools.partial(_pre_kernel, alpha=alpha),
        grid=(rows // tm,),
        in_specs=[_row_spec(tm, d)] + _slab_specs(slabs),
        out_specs=_row_spec(tm, d),
        out_shape=jax.ShapeDtypeStruct((rows, d), F32),
        compiler_params=_row_params(),
        name="pre_ffn",
    )(x, *_arrays(slabs))


def _pre_qkv_call(x, slabs, k_all, v_all, j, n_sb, alpha, sb_scale):
    rows, d = x.shape
    tm = min(ROW_TILE, rows)
    kv_spec = pl.BlockSpec((None, tm, d), lambda i: (j, i, 0))
    args = [x] + _arrays(slabs)
    specs = [_row_spec(tm, d)] + _slab_specs(slabs)
    aliases = {}
    if k_all is not None:
        aliases = {len(args): 2, len(args) + 1: 3}
        args += [k_all, v_all]
        specs += [pl.BlockSpec(memory_space=pl.ANY)] * 2
    return pl.pallas_call(
        functools.partial(_pre_qkv_kernel, alpha=alpha, sb_scale=sb_scale),
        grid=(rows // tm,),
        in_specs=specs,
        out_specs=[_row_spec(tm, d), _row_spec(tm, d), kv_spec, kv_spec],
        out_shape=[jax.ShapeDtypeStruct((rows, d), F32), jax.ShapeDtypeStruct((rows, d), F32),
                   jax.ShapeDtypeStruct((n_sb, rows, d), F32),
                   jax.ShapeDtypeStruct((n_sb, rows, d), F32)],
        input_output_aliases=aliases,
        compiler_params=_row_params(),
        name="pre_ffn_qkv",
    )(*args)


def _post_call(x1, m, p_all, i, w_out, slabs, alpha):
    rows, d = x1.shape
    tm = min(ROW_TILE, rows)
    has_wout = w_out is not None
    slabs = ([w_out] if has_wout else []) + list(slabs)
    args = [x1, m, p_all] + _arrays(slabs)
    specs = [_row_spec(tm, d), _row_spec(tm, d),
             pl.BlockSpec((None, tm, p_all.shape[2]), lambda r: (i, r, 0))]
    specs += _slab_specs(slabs)
    return pl.pallas_call(
        functools.partial(_post_kernel, alpha=alpha, has_wout=has_wout),
        grid=(rows // tm,),
        in_specs=specs,
        out_specs=_row_spec(tm, d),
        out_shape=jax.ShapeDtypeStruct((rows, d), F32),
        compiler_params=_row_params(),
        name="post_mixer",
    )(*args)


def _post_pool_call(x1, seq, p_all, i, slabs, alpha):
    rows, d = x1.shape
    tm = min(ROW_TILE, seq)
    tiles_per_seq = seq // tm
    specs = [_row_spec(tm, d), pl.BlockSpec((None, tm, p_all.shape[2]), lambda r: (i, r, 0))]
    specs += _slab_specs(slabs)
    return pl.pallas_call(
        functools.partial(_post_pool_kernel, alpha=alpha, tiles_per_seq=tiles_per_seq),
        grid=(rows // tm,),
        in_specs=specs,
        out_specs=[_row_spec(tm, d),
                   pl.BlockSpec((1, POOL_MAX - 1, d), lambda r: (r // tiles_per_seq, 0, 0))],
        out_shape=[jax.ShapeDtypeStruct((rows, d), F32),
                   jax.ShapeDtypeStruct((rows // seq, POOL_MAX - 1, d), F32)],
        scratch_shapes=[pltpu.VMEM((tm + POOL_MAX, d), F32)],
        compiler_params=_row_params(),
        name="post_pool_mixer",
    )(x1, p_all, *_arrays(slabs))


def _pool_kernel(x_ref, prev_ref, pw_ref, scale_ref, m_ref, np_ref, xe_ref, *, pos0):
    tm = x_ref.shape[1]
    x = x_ref[0]
    xe_ref[0:POOL_MAX, :] = prev_ref[0]
    xe_ref[POOL_MAX:POOL_MAX + tm, :] = x
    pos = pos0 + lax.broadcasted_iota(jnp.int32, (tm, 1), 0)
    m_ref[0] = _pool_mix(x, xe_ref, pos, pw_ref, scale_ref)
    np_ref[0] = x[tm - (POOL_MAX - 1):tm, :]


def _pool_call(x, prev16, pw, scale, pos0):
    bsz, s, d = x.shape
    return pl.pallas_call(
        functools.partial(_pool_kernel, pos0=pos0),
        grid=(bsz,),
        in_specs=[pl.BlockSpec((1, s, d), lambda b: (b, 0, 0)),
                  pl.BlockSpec((1, POOL_MAX, d), lambda b: (b, 0, 0))] + _slab_specs([pw, scale]),
        out_specs=[pl.BlockSpec((1, s, d), lambda b: (b, 0, 0)),
                   pl.BlockSpec((1, POOL_MAX - 1, d), lambda b: (b, 0, 0))],
        out_shape=[jax.ShapeDtypeStruct((bsz, s, d), F32),
                   jax.ShapeDtypeStruct((bsz, POOL_MAX - 1, d), F32)],
        scratch_shapes=[pltpu.VMEM((s + POOL_MAX, d), F32)],
        compiler_params=_row_params(),
        name="pool_mixer",
    )(x, prev16, *_arrays([pw, scale]))


def _tri_strict(t):
    r = lax.broadcasted_iota(jnp.int32, (t, t), 0)
    c = lax.broadcasted_iota(jnp.int32, (t, t), 1)
    return jnp.where(r > c, 1.0, 0.0)


def _softplus(z):
    return jnp.maximum(z, jnp.log(1.0 + jnp.exp(jnp.minimum(z, SOFTPLUS_CLAMP))))


def _sb_block(z, v, tri, carry, mask):
    sp = _softplus(z)
    ls = z - sp
    if mask is not None:
        sp = jnp.where(mask, sp, 0.0)
    cs = _dot(sp, tri)
    w = jnp.exp(ls - cs - carry)
    if mask is not None:
        w = jnp.where(mask, w, 0.0)
    return _dot(w, v), carry + cs[:, 0:1] + sp[:, 0:1]


def _head_split(q):
    lane = lax.broadcasted_iota(jnp.int32, (1, LANES), 1)
    zero = jnp.zeros_like(q)
    return [jnp.where(lane < LANES // 2, q, zero), jnp.where(lane >= LANES // 2, q, zero)]


def _head_merge(o0, o1):
    lane = lax.broadcasted_iota(jnp.int32, (1, LANES), 1)
    return jnp.where(lane < LANES // 2, o0, o1)


def _attn_prompt_kernel(q_ref, k_ref, v_ref, o_ref, tri_ref, bias_ref,
                        ls0_ref, ls1_ref, sp0_ref, sp1_ref, w0_ref, w1_ref,
                        carry_ref, low_ref, tot_ref, *, seq, tile):
    nblk = seq // tile
    assert nblk >= 2
    n_items = 2 * nblk - 1
    sink_blk = nblk
    ls_refs, sp_refs, w_refs = (ls0_ref, ls1_ref), (sp0_ref, sp1_ref), (w0_ref, w1_ref)

    @pl.when((pl.program_id(0) == 0) & (pl.program_id(1) == 0))
    def _():
        r = lax.broadcasted_iota(jnp.int32, (tile, tile), 0)
        c = lax.broadcasted_iota(jnp.int32, (tile, tile), 1)
        tri_ref[...] = _tri_strict(tile)
        bias_ref[...] = jnp.where(c < r, 0.0, MASK_BIAS)

    low_ref[...] = jnp.full(low_ref.shape, SKIP_MASS, F32)

    def rows(i):
        return pl.ds(pl.multiple_of(i * tile, tile), tile)

    def item(n, diag):
        qi = jnp.minimum((n + 1) // 2, nblk - 1)
        return qi, (qi if diag else qi - 1)

    def stage_a(n, slot, diag):
        qi, kj = item(n, diag)
        k = k_ref[0, rows(kj), :]
        for h, qh in enumerate(_head_split(q_ref[0, rows(qi), :])):
            z = _dot_nt(qh, k)
            if diag:
                z = z + bias_ref[...]
            sp = _softplus(z)
            sp_refs[slot][h] = sp
            ls_refs[slot][h] = z - sp

    def stage_b(n, slot, diag):
        qi, _ = item(n, diag)
        tot_idx = jnp.where(n < n_items, qi, sink_blk)
        tri = tri_ref[...]
        for h in range(2):
            sp = sp_refs[slot][h]
            cs = _dot(sp, tri)
            e = ls_refs[slot][h] - cs
            mass = cs[:, 0:1] + sp[:, 0:1]
            if not diag:
                c_in = carry_ref[h][:, 0:1]
                e = e - c_in
                mass = mass + c_in
            w_refs[slot][h] = jnp.exp(e)
            mass = jnp.broadcast_to(mass, (tile, LANES))
            if diag:
                carry_ref[h] = mass
            else:
                low_ref[h] = jnp.minimum(low_ref[h], jnp.where(qi >= 2, mass, SKIP_MASS))
            tot_ref[tot_idx, h] = mass

    def stage_c(n, slot, diag):
        qi, kj = item(n, diag)
        v = v_ref[0, rows(kj), :]
        pv = _head_merge(_dot(w_refs[slot][0], v), _dot(w_refs[slot][1], v))
        if diag:
            o_ref[0, rows(qi), :] = pv
        else:
            o_ref[0, rows(qi), :] += pv

    stage_a(0, 0, True)
    stage_b(0, 0, True)
    stage_a(1, 1, True)
    stage_c(0, 0, True)
    stage_b(1, 1, True)
    stage_a(2, 0, False)

    def pair_body(m, _):
        n = 2 * m + 3
        stage_c(n - 2, 1, True)
        stage_b(n - 1, 0, False)
        stage_a(n, 1, True)
        stage_c(n - 1, 0, False)
        stage_b(n, 1, True)
        stage_a(n + 1, 0, False)
        return 0

    lax.fori_loop(0, nblk - 1, pair_body, 0)

    def tail_body(qi, _):
        @pl.when(jnp.min(tot_ref[qi]) < SKIP_MASS)
        def _():
            qh = _head_split(q_ref[0, rows(qi), :])
            tri = tri_ref[...]

            def cond(state):
                kj, c0, c1 = state
                return jnp.logical_and(kj >= 0, jnp.minimum(jnp.min(c0), jnp.min(c1)) < SKIP_MASS)

            def body(state):
                kj, c0, c1 = state
                k = k_ref[0, rows(kj), :]
                v = v_ref[0, rows(kj), :]
                pv0, c0 = _sb_block(_dot_nt(qh[0], k), v, tri, c0, None)
                pv1, c1 = _sb_block(_dot_nt(qh[1], k), v, tri, c1, None)
                o_ref[0, rows(qi), :] += _head_merge(pv0, pv1)
                return (kj - 1, c0, c1)

            lax.while_loop(cond, body, (qi - 2, tot_ref[qi, 0][:, 0:1], tot_ref[qi, 1][:, 0:1]))
        return 0

    if nblk > 2:
        @pl.when(jnp.min(low_ref[...]) < SKIP_MASS)
        def _():
            lax.fori_loop(2, nblk, tail_body, 0)


def _attn_prompt_call(q, k_all, v_all, j):
    bsz, s, d = q.shape
    tile = min(ATT_TILE, s)
    nblk = s // tile
    spec = pl.BlockSpec((1, s, LANES), lambda b, h: (b, 0, h))
    kv_spec = pl.BlockSpec((None, 1, s, LANES), lambda b, h: (j, b, 0, h))
    return pl.pallas_call(
        functools.partial(_attn_prompt_kernel, seq=s, tile=tile),
        grid=(bsz, d // LANES),
        in_specs=[spec, kv_spec, kv_spec],
        out_specs=spec,
        out_shape=jax.ShapeDtypeStruct((bsz, s, d), F32),
        scratch_shapes=[pltpu.VMEM((tile, tile), F32),
                        pltpu.VMEM((tile, tile), F32)]
                       + [pltpu.VMEM((2, tile, tile), F32)] * 6
                       + [pltpu.VMEM((2, tile, LANES), F32),
                          pltpu.VMEM((2, tile, LANES), F32),
                          pltpu.VMEM((nblk + 1, 2, tile, LANES), F32)],
        compiler_params=pltpu.CompilerParams(dimension_semantics=("arbitrary", "arbitrary"),
                                             vmem_limit_bytes=VMEM_LIMIT),
        name="sb_attn_prompt",
    )(q, k_all, v_all)


def _attn_sample_kernel(*refs, n_new, tile, blk, resume):
    if resume:
        q_ref, acc_in_ref, mass_in_ref, ck_ref, cv_ref, o_ref, acc_ref, mass_ref, qbd_ref = refs
    else:
        q_ref, kn_ref, vn_ref, ck_ref, cv_ref, o_ref, acc_ref, mass_ref, qbd_ref = refs
    t = pl.program_id(1)
    d = q_ref.shape[2]
    hd = d // N_HEADS
    m = N_HEADS * n_new
    tri = _tri_strict(tile)

    @pl.when(t == 0)
    def _():
        q = q_ref[0]
        col = lax.broadcasted_iota(jnp.int32, (n_new, d), 1)
        for h in range(N_HEADS):
            qbd_ref[h * n_new:(h + 1) * n_new, :] = jnp.where(col // hd == h, q, jnp.zeros_like(q))
        if resume:
            acc_ref[0] = acc_in_ref[0]
            mass_ref[0] = mass_in_ref[0]
        else:
            t_idx = lax.broadcasted_iota(jnp.int32, (m, tile), 0) % n_new
            s_idx = lax.broadcasted_iota(jnp.int32, (m, tile), 1)
            z = _dot_nt(qbd_ref[...], kn_ref[0])
            pv, cr = _sb_block(z, vn_ref[0], tri, jnp.zeros((m, 1), F32), s_idx < t_idx)
            acc_ref[0] = pv
            mass_ref[0] = jnp.broadcast_to(cr, (m, LANES))

    qbd = qbd_ref[...]
    for i in reversed(range(blk // tile)):
        kb = ck_ref[0, i * tile:(i + 1) * tile, :]
        vb = cv_ref[0, i * tile:(i + 1) * tile, :]
        pv, cr = _sb_block(_dot_nt(qbd, kb), vb, tri, mass_ref[0][:, 0:1], None)
        acc_ref[0] += pv
        mass_ref[0] = jnp.broadcast_to(cr, (m, LANES))

    @pl.when(t == pl.num_programs(1) - 1)
    def _():
        col = lax.broadcasted_iota(jnp.int32, (n_new, d), 1)
        o = jnp.zeros((n_new, d), F32)
        for h in range(N_HEADS):
            o = o + jnp.where(col // hd == h, acc_ref[0, h * n_new:(h + 1) * n_new, :], 0.0)
        o_ref[0] = o


def _attn_sample_call(q, first, caches, cache_spec, nkb, blk, resume):
    bsz, n_new, d = q.shape
    m = N_HEADS * n_new
    q_spec = pl.BlockSpec((1, n_new, d), lambda b, t: (b, 0, 0))
    acc_spec = pl.BlockSpec((1, m, d), lambda b, t: (b, 0, 0))
    mass_spec = pl.BlockSpec((1, m, LANES), lambda b, t: (b, 0, 0))
    if resume:
        first_specs = [acc_spec, mass_spec]
    else:
        first_specs = [pl.BlockSpec((1, ATT_TILE, d), lambda b, t: (b, 0, 0))] * 2
    return pl.pallas_call(
        functools.partial(_attn_sample_kernel, n_new=n_new, tile=ATT_TILE, blk=blk, resume=resume),
        grid=(bsz, nkb),
        in_specs=[q_spec] + first_specs + [cache_spec, cache_spec],
        out_specs=[q_spec, acc_spec, mass_spec],
        out_shape=[jax.ShapeDtypeStruct((bsz, n_new, d), F32),
                   jax.ShapeDtypeStruct((bsz, m, d), F32),
                   jax.ShapeDtypeStruct((bsz, m, LANES), F32)],
        scratch_shapes=[pltpu.VMEM((m, d), F32)],
        compiler_params=pltpu.CompilerParams(dimension_semantics=("arbitrary", "arbitrary"),
                                             vmem_limit_bytes=VMEM_LIMIT),
        name="sb_attn_sample",
    )(q, *first, *caches)


def _attn_sample(q, k_new, v_new, cache_k, cache_v, j):
    bsz, n_new, d = q.shape
    n_sb, _, past = cache_k.shape[:3]
    tile = ATT_TILE
    assert past % tile == 0 and n_new <= tile
    pad = ((0, 0), (0, tile - n_new), (0, 0))
    tail_k = cache_k[j, :, past - tile:].reshape(bsz, tile, d)
    tail_v = cache_v[j, :, past - tile:].reshape(bsz, tile, d)
    tail_spec = pl.BlockSpec((1, tile, d), lambda b, t: (b, 0, 0))
    o, acc, mass = _attn_sample_call(q, (jnp.pad(k_new, pad), jnp.pad(v_new, pad)),
                                     (tail_k, tail_v), tail_spec, 1, tile, False)
    n_rest = past // tile - 1
    if n_rest == 0:
        return o

    def older_keys(q, acc, mass, cache_k, cache_v):
        ck = cache_k.reshape(n_sb, bsz, past, d)
        cv = cache_v.reshape(n_sb, bsz, past, d)
        rest_spec = pl.BlockSpec((None, 1, tile, d), lambda b, t: (j, b, n_rest - 1 - t, 0))
        return _attn_sample_call(q, (acc, mass), (ck, cv), rest_spec, n_rest, tile, True)[0]

    return lax.cond(jnp.min(mass) < SKIP_MASS,
                    lambda o, *rest: older_keys(*rest), lambda o, *rest: o,
                    o, q, acc, mass, cache_k, cache_v)


def kernel(x_prompt, x_sample, cache_k, cache_v, state_pool, p_prompt, p_sample,
           ln_g, ln_b, ffn_w1, ffn_w3, ffn_w2, pool_w, pool_scale,
           sb_w_in, sb_w_out, ple_w_gate, ple_b_gate, ple_w_proj):
    depth = ln_g.shape[0]
    bp, sp, d = x_prompt.shape
    bs, ss, _ = x_sample.shape
    n_sb, _, past, heads, hd = cache_k.shape
    assert heads == N_HEADS and hd * heads == d and LANES == 2 * hd
    alpha = (2.0 * depth) ** 0.25
    sb_scale = 1.0 / math.sqrt(hd)

    w1 = ffn_w1.astype(BF16)
    w3 = ffn_w3.astype(BF16)
    w2 = ffn_w2.astype(BF16)
    pw = pool_w.astype(BF16)
    w_in = sb_w_in.astype(BF16)
    w_out = sb_w_out.astype(BF16)
    wg = ple_w_gate.astype(BF16)
    wp = ple_w_proj.astype(BF16)
    bg = ple_b_gate[:, None, :]
    scale = pool_scale[:, None, :]

    xp = x_prompt.reshape(bp * sp, d)
    xs = x_sample.reshape(bs * ss, d)
    pp = p_prompt.reshape(depth, bp * sp, -1)
    ps = p_sample.reshape(depth, bs * ss, -1)

    kp_all = vp_all = ks_all = vs_all = None
    pool_p, pool_s = [], []

    for i in range(depth):
        j = i // 2
        norms = [(ln_g, (i,)), (ln_b, (i,))]
        ffn0 = [(w1, (i, 0)), (w3, (i, 0)), (w2, (i, 0))]
        tail = norms + [(w1, (i, 1)), (w3, (i, 1)), (w2, (i, 1)), (wg, (i,)), (bg, (i,)), (wp, (i,))]
        if i % 2 == 0:
            xp1 = _pre_call(xp, ffn0 + norms, alpha)
            xs1 = _pre_call(xs, ffn0 + norms, alpha)
            pool = [(pw, (j,)), (scale, (j,))]
            xp, npp = _post_pool_call(xp1, sp, pp, i, pool + tail, alpha)
            prev_s = jnp.pad(state_pool[j], ((0, 0), (1, 0), (0, 0)))
            ms, nps = _pool_call(xs1.reshape(bs, ss, d), prev_s, *pool, past)
            xs = _post_call(xs1, ms.reshape(bs * ss, d), ps, i, None, tail, alpha)
            pool_p.append(npp)
            pool_s.append(nps)
        else:
            qkv = ffn0 + norms + [(w_in, (j,))]
            xp1, qp, kp_all, vp_all = _pre_qkv_call(xp, qkv, kp_all, vp_all, j, n_sb, alpha, sb_scale)
            xs1, qs, ks_all, vs_all = _pre_qkv_call(xs, qkv, ks_all, vs_all, j, n_sb, alpha, sb_scale)
            op = _attn_prompt_call(qp.reshape(bp, sp, d), kp_all.reshape(n_sb, bp, sp, d),
                                   vp_all.reshape(n_sb, bp, sp, d), j)
            osm = _attn_sample(qs.reshape(bs, ss, d), ks_all[j].reshape(bs, ss, d),
                               vs_all[j].reshape(bs, ss, d), cache_k, cache_v, j)
            xp = _post_call(xp1, op.reshape(bp * sp, d), pp, i, (w_out, (j,)), tail, alpha)
            xs = _post_call(xs1, osm.reshape(bs * ss, d), ps, i, (w_out, (j,)), tail, alpha)

    return (xp.reshape(bp, sp, d), xs.reshape(bs, ss, d),
            kp_all.reshape(n_sb, bp, sp, heads, hd), vp_all.reshape(n_sb, bp, sp, heads, hd),
            ks_all.reshape(n_sb, bs, ss, heads, hd), vs_all.reshape(n_sb, bs, ss, heads, hd),
            jnp.stack(pool_p), jnp.stack(pool_s))
```

```python
import functools
import math

import jax
import jax.numpy as jnp
from jax import lax
from jax.experimental import pallas as pl
from jax.experimental.pallas import tpu as pltpu

N_HEADS = 16
POOL_WINDOWS = (2, 4, 8, 16)
POOL_MAX = max(POOL_WINDOWS)
LN_EPS = 1e-5

LANES = 128
VMEM_LIMIT = 52 * 1024 * 1024
ROW_TILE = 512
FF_CHUNK = 256
ATT_TILE = 256
MASK_BIAS = -1e30
SKIP_MASS = 104.0
SOFTPLUS_CLAMP = 80.0

BF16 = jnp.bfloat16
F32 = jnp.float32


def _dot(a, b):
    return jnp.dot(a, b, preferred_element_type=F32)


def _dot_nt(a, b):
    return lax.dot_general(a, b, (((1,), (1,)), ((), ())), preferred_element_type=F32)


def _ln(x, g, b):
    mu = jnp.mean(x, axis=-1, keepdims=True)
    xc = x - mu
    var = jnp.mean(xc * xc, axis=-1, keepdims=True)
    return xc * lax.rsqrt(var + LN_EPS) * g + b


def _half_ffn(x, w1_ref, w3_ref, w2_ref, g, b, alpha):
    d_ff = w1_ref.shape[1]
    xb = x.astype(BF16)
    acc = None
    for c0 in range(0, d_ff, FF_CHUNK):
        c1 = min(c0 + FF_CHUNK, d_ff)
        a = _dot(xb, w1_ref[:, c0:c1])
        u = _dot(xb, w3_ref[:, c0:c1])
        h = (a * jax.nn.sigmoid(a) * u).astype(BF16)
        part = _dot(h, w2_ref[c0:c1, :])
        acc = part if acc is None else acc + part
    return _ln(alpha * x + 0.5 * acc, g, b)


def _post_chain(x1, m, p, g_ref, b_ref, w1_ref, w3_ref, w2_ref, wg_ref, bg_ref, wp_ref, alpha):
    x2 = _ln(alpha * x1 + m, g_ref[1:2, :], b_ref[1:2, :])
    x3 = _half_ffn(x2, w1_ref, w3_ref, w2_ref, g_ref[2:3, :], b_ref[2:3, :], alpha)
    gate = jax.nn.sigmoid(_dot(x3.astype(BF16), wg_ref[...]) + bg_ref[...])
    e = _dot(p.astype(BF16), wp_ref[...])
    return _ln(alpha * x3 + gate * e, g_ref[3:4, :], b_ref[3:4, :])


def _pool_mix(x, xe_ref, pos, pw_ref, scale_ref):
    gw = x.shape[1] // len(POOL_WINDOWS)
    parts = [_pool_group(x[:, g * gw:(g + 1) * gw], xe_ref, pos, pw_ref, scale_ref, g, 0)
             for g in range(len(POOL_WINDOWS))]
    return jnp.concatenate(parts, axis=1)


def _pool_group(xg, xe_ref, pos, pw_ref, scale_ref, g, r0):
    n, gw = xg.shape
    w = POOL_WINDOWS[g]
    c0, c1 = g * gw, (g + 1) * gw
    s = xg
    for k in range(1, w):
        s = s + xe_ref[POOL_MAX - k + r0:POOL_MAX - k + r0 + n, c0:c1]
    cnt = jnp.minimum(w, pos + 1).astype(F32)
    y = (s / cnt - xg).astype(BF16)
    return _dot(y, pw_ref[g]) * scale_ref[:, c0:c1]


def _pre_kernel(x_ref, w1_ref, w3_ref, w2_ref, g_ref, b_ref, o_ref, *, alpha):
    o_ref[...] = _half_ffn(x_ref[...], w1_ref, w3_ref, w2_ref, g_ref[0:1, :], b_ref[0:1, :], alpha)


def _pre_qkv_kernel(x_ref, w1_ref, w3_ref, w2_ref, g_ref, b_ref, win_ref, *refs, alpha, sb_scale):
    x1_ref, q_ref, k_ref, v_ref = refs[-4:]
    d = x_ref.shape[1]
    x1 = _half_ffn(x_ref[...], w1_ref, w3_ref, w2_ref, g_ref[0:1, :], b_ref[0:1, :], alpha)
    x1_ref[...] = x1
    xb = x1.astype(BF16)
    q_ref[...] = _dot(xb, win_ref[:, 0:d]) * sb_scale
    k_ref[...] = _dot(xb, win_ref[:, d:2 * d])
    v_ref[...] = _dot(xb, win_ref[:, 2 * d:3 * d])


def _post_kernel(*refs, alpha, has_wout):
    if has_wout:
        x1_ref, m_ref, p_ref, wout_ref = refs[:4]
        m = _dot(m_ref[...].astype(BF16), wout_ref[...])
    else:
        x1_ref, m_ref, p_ref = refs[:3]
        m = m_ref[...]
    o_ref = refs[-1]
    o_ref[...] = _post_chain(x1_ref[...], m, p_ref[...], *refs[-9:-1], alpha)


def _post_pool_kernel(x1_ref, p_ref, pw_ref, scale_ref, g_ref, b_ref, w1_ref, w3_ref, w2_ref,
                      wg_ref, bg_ref, wp_ref, o_ref, np_ref, xe_ref, *, alpha, tiles_per_seq):
    tm = x1_ref.shape[0]
    t = pl.program_id(0) % tiles_per_seq

    @pl.when(t == 0)
    def _():
        xe_ref[0:POOL_MAX, :] = jnp.zeros((POOL_MAX, xe_ref.shape[1]), F32)

    @pl.when(t > 0)
    def _():
        xe_ref[0:POOL_MAX, :] = xe_ref[tm:tm + POOL_MAX, :]

    x1 = x1_ref[...]
    xe_ref[POOL_MAX:POOL_MAX + tm, :] = x1
    pos = t * tm + lax.broadcasted_iota(jnp.int32, (tm, 1), 0)
    m = _pool_mix(x1, xe_ref, pos, pw_ref, scale_ref)
    np_ref[0] = x1[tm - (POOL_MAX - 1):tm, :]
    o_ref[...] = _post_chain(x1, m, p_ref[...], g_ref, b_ref, w1_ref, w3_ref, w2_ref,
                             wg_ref, bg_ref, wp_ref, alpha)


def _slab_specs(slabs):
    specs = []
    for arr, lead in slabs:
        shape = (None,) * len(lead) + tuple(arr.shape[len(lead):])
        idx = tuple(lead) + (0,) * (arr.ndim - len(lead))
        specs.append(pl.BlockSpec(shape, lambda *_, idx=idx: idx, pipeline_mode=pl.Buffered(1)))
    return specs


def _arrays(slabs):
    return [arr for arr, _ in slabs]


def _row_spec(tm, width):
    return pl.BlockSpec((tm, width), lambda i: (i, 0))


def _row_params():
    return pltpu.CompilerParams(dimension_semantics=("arbitrary",), vmem_limit_bytes=VMEM_LIMIT)


def _pre_call(x, slabs, alpha):
    rows, d = x.shape
    tm = min(ROW_TILE, rows)
    return pl.pallas_call(
        functools.partial(_pre_kernel, alpha=alpha),
        grid=(rows // tm,),
        in_specs=[_row_spec(tm, d)] + _slab_specs(slabs),
        out_specs=_row_spec(tm, d),
        out_shape=jax.ShapeDtypeStruct((rows, d), F32),
        compiler_params=_row_params(),
        name="pre_ffn",
    )(x, *_arrays(slabs))


def _pre_qkv_call(x, slabs, k_all, v_all, j, n_sb, alpha, sb_scale):
    rows, d = x.shape
    tm = min(ROW_TILE, rows)
    kv_spec = pl.BlockSpec((None, tm, d), lambda i: (j, i, 0))
    args = [x] + _arrays(slabs)
    specs = [_row_spec(tm, d)] + _slab_specs(slabs)
    aliases = {}
    if k_all is not None:
        aliases = {len(args): 2, len(args) + 1: 3}
        args += [k_all, v_all]
        specs += [pl.BlockSpec(memory_space=pl.ANY)] * 2
    return pl.pallas_call(
        functools.partial(_pre_qkv_kernel, alpha=alpha, sb_scale=sb_scale),
        grid=(rows // tm,),
        in_specs=specs,
        out_specs=[_row_spec(tm, d), _row_spec(tm, d), kv_spec, kv_spec],
        out_shape=[jax.ShapeDtypeStruct((rows, d), F32), jax.ShapeDtypeStruct((rows, d), F32),
                   jax.ShapeDtypeStruct((n_sb, rows, d), F32),
                   jax.ShapeDtypeStruct((n_sb, rows, d), F32)],
        input_output_aliases=aliases,
        compiler_params=_row_params(),
        name="pre_ffn_qkv",
    )(*args)


def _post_call(x1, m, p_all, i, w_out, slabs, alpha):
    rows, d = x1.shape
    tm = min(ROW_TILE, rows)
    has_wout = w_out is not None
    slabs = ([w_out] if has_wout else []) + list(slabs)
    args = [x1, m, p_all] + _arrays(slabs)
    specs = [_row_spec(tm, d), _row_spec(tm, d),
             pl.BlockSpec((None, tm, p_all.shape[2]), lambda r: (i, r, 0))]
    specs += _slab_specs(slabs)
    return pl.pallas_call(
        functools.partial(_post_kernel, alpha=alpha, has_wout=has_wout),
        grid=(rows // tm,),
        in_specs=specs,
        out_specs=_row_spec(tm, d),
        out_shape=jax.ShapeDtypeStruct((rows, d), F32),
        compiler_params=_row_params(),
        name="post_mixer",
    )(*args)


def _post_pool_call(x1, seq, p_all, i, slabs, alpha):
    rows, d = x1.shape
    tm = min(ROW_TILE, seq)
    tiles_per_seq = seq // tm
    specs = [_row_spec(tm, d), pl.BlockSpec((None, tm, p_all.shape[2]), lambda r: (i, r, 0))]
    specs += _slab_specs(slabs)
    return pl.pallas_call(
        functools.partial(_post_pool_kernel, alpha=alpha, tiles_per_seq=tiles_per_seq),
        grid=(rows // tm,),
        in_specs=specs,
        out_specs=[_row_spec(tm, d),
                   pl.BlockSpec((1, POOL_MAX - 1, d), lambda r: (r // tiles_per_seq, 0, 0))],
        out_shape=[jax.ShapeDtypeStruct((rows, d), F32),
                   jax.ShapeDtypeStruct((rows // seq, POOL_MAX - 1, d), F32)],
        scratch_shapes=[pltpu.VMEM((tm + POOL_MAX, d), F32)],
        compiler_params=_row_params(),
        name="post_pool_mixer",
    )(x1, p_all, *_arrays(slabs))


def _pool_kernel(x_ref, prev_ref, pw_ref, scale_ref, m_ref, np_ref, xe_ref, *, pos0):
    tm = x_ref.shape[1]
    x = x_ref[0]
    xe_ref[0:POOL_MAX, :] = prev_ref[0]
    xe_ref[POOL_MAX:POOL_MAX + tm, :] = x
    pos = pos0 + lax.broadcasted_iota(jnp.int32, (tm, 1), 0)
    m_ref[0] = _pool_mix(x, xe_ref, pos, pw_ref, scale_ref)
    np_ref[0] = x[tm - (POOL_MAX - 1):tm, :]


def _pool_call(x, prev16, pw, scale, pos0):
    bsz, s, d = x.shape
    return pl.pallas_call(
        functools.partial(_pool_kernel, pos0=pos0),
        grid=(bsz,),
        in_specs=[pl.BlockSpec((1, s, d), lambda b: (b, 0, 0)),
                  pl.BlockSpec((1, POOL_MAX, d), lambda b: (b, 0, 0))] + _slab_specs([pw, scale]),
        out_specs=[pl.BlockSpec((1, s, d), lambda b: (b, 0, 0)),
                   pl.BlockSpec((1, POOL_MAX - 1, d), lambda b: (b, 0, 0))],
        out_shape=[jax.ShapeDtypeStruct((bsz, s, d), F32),
                   jax.ShapeDtypeStruct((bsz, POOL_MAX - 1, d), F32)],
        scratch_shapes=[pltpu.VMEM((s + POOL_MAX, d), F32)],
        compiler_params=_row_params(),
        name="pool_mixer",
    )(x, prev16, *_arrays([pw, scale]))


def _tri_strict(t):
    r = lax.broadcasted_iota(jnp.int32, (t, t), 0)
    c = lax.broadcasted_iota(jnp.int32, (t, t), 1)
    return jnp.where(r > c, 1.0, 0.0)


def _softplus(z):
    return jnp.maximum(z, jnp.log(1.0 + jnp.exp(jnp.minimum(z, SOFTPLUS_CLAMP))))


def _sb_block(z, v, tri, carry, mask):
    sp = _softplus(z)
    ls = z - sp
    if mask is not None:
        sp = jnp.where(mask, sp, 0.0)
    cs = _dot(sp, tri)
    w = jnp.exp(ls - cs - carry)
    if mask is not None:
        w = jnp.where(mask, w, 0.0)
    return _dot(w, v), carry + cs[:, 0:1] + sp[:, 0:1]


def _head_split(q):
    lane = lax.broadcasted_iota(jnp.int32, (1, LANES), 1)
    zero = jnp.zeros_like(q)
    return [jnp.where(lane < LANES // 2, q, zero), jnp.where(lane >= LANES // 2, q, zero)]


def _head_merge(o0, o1):
    lane = lax.broadcasted_iota(jnp.int32, (1, LANES), 1)
    return jnp.where(lane < LANES // 2, o0, o1)


def _attn_prompt_kernel(q_ref, k_ref, v_ref, o_ref, tri_ref, bias_ref,
                        ls0_ref, ls1_ref, sp0_ref, sp1_ref, w0_ref, w1_ref,
                        carry_ref, low_ref, tot_ref, *, seq, tile):
    nblk = seq // tile
    assert nblk >= 2
    n_items = 2 * nblk - 1
    sink_blk = nblk
    ls_refs, sp_refs, w_refs = (ls0_ref, ls1_ref), (sp0_ref, sp1_ref), (w0_ref, w1_ref)

    @pl.when((pl.program_id(0) == 0) & (pl.program_id(1) == 0))
    def _():
        r = lax.broadcasted_iota(jnp.int32, (tile, tile), 0)
        c = lax.broadcasted_iota(jnp.int32, (tile, tile), 1)
        tri_ref[...] = _tri_strict(tile)
        bias_ref[...] = jnp.where(c < r, 0.0, MASK_BIAS)

    low_ref[...] = jnp.full(low_ref.shape, SKIP_MASS, F32)

    def rows(i):
        return pl.ds(pl.multiple_of(i * tile, tile), tile)

    def item(n, diag):
        qi = jnp.minimum((n + 1) // 2, nblk - 1)
        return qi, (qi if diag else qi - 1)

    def stage_a(n, slot, diag):
        qi, kj = item(n, diag)
        k = k_ref[0, rows(kj), :]
        for h, qh in enumerate(_head_split(q_ref[0, rows(qi), :])):
            z = _dot_nt(qh, k)
            if diag:
                z = z + bias_ref[...]
            sp = _softplus(z)
            sp_refs[slot][h] = sp
            ls_refs[slot][h] = z - sp

    def stage_b(n, slot, diag):
        qi, _ = item(n, diag)
        tot_idx = jnp.where(n < n_items, qi, sink_blk)
        tri = tri_ref[...]
        for h in range(2):
            sp = sp_refs[slot][h]
            cs = _dot(sp, tri)
            e = ls_refs[slot][h] - cs
            mass = cs[:, 0:1] + sp[:, 0:1]
            if not diag:
                c_in = carry_ref[h][:, 0:1]
                e = e - c_in
                mass = mass + c_in
            w_refs[slot][h] = jnp.exp(e)
            mass = jnp.broadcast_to(mass, (tile, LANES))
            if diag:
                carry_ref[h] = mass
            else:
                low_ref[h] = jnp.minimum(low_ref[h], jnp.where(qi >= 2, mass, SKIP_MASS))
            tot_ref[tot_idx, h] = mass

    def stage_c(n, slot, diag):
        qi, kj = item(n, diag)
        v = v_ref[0, rows(kj), :]
        pv = _head_merge(_dot(w_refs[slot][0], v), _dot(w_refs[slot][1], v))
        if diag:
            o_ref[0, rows(qi), :] = pv
        else:
            o_ref[0, rows(qi), :] += pv

    stage_a(0, 0, True)
    stage_b(0, 0, True)
    stage_a(1, 1, True)
    stage_c(0, 0, True)
    stage_b(1, 1, True)
    stage_a(2, 0, False)

    def pair_body(m, _):
        n = 2 * m + 3
        stage_c(n - 2, 1, True)
        stage_b(n - 1, 0, False)
        stage_a(n, 1, True)
        stage_c(n - 1, 0, False)
        stage_b(n, 1, True)
        stage_a(n + 1, 0, False)
        return 0

    lax.fori_loop(0, nblk - 1, pair_body, 0)

    def tail_body(qi, _):
        @pl.when(jnp.min(tot_ref[qi]) < SKIP_MASS)
        def _():
            qh = _head_split(q_ref[0, rows(qi), :])
            tri = tri_ref[...]

            def cond(state):
                kj, c0, c1 = state
                return jnp.logical_and(kj >= 0, jnp.minimum(jnp.min(c0), jnp.min(c1)) < SKIP_MASS)

            def body(state):
                kj, c0, c1 = state
                k = k_ref[0, rows(kj), :]
                v = v_ref[0, rows(kj), :]
                pv0, c0 = _sb_block(_dot_nt(qh[0], k), v, tri, c0, None)
                pv1, c1 = _sb_block(_dot_nt(qh[1], k), v, tri, c1, None)
                o_ref[0, rows(qi), :] += _head_merge(pv0, pv1)
                return (kj - 1, c0, c1)

            lax.while_loop(cond, body, (qi - 2, tot_ref[qi, 0][:, 0:1], tot_ref[qi, 1][:, 0:1]))
        return 0

    if nblk > 2:
        @pl.when(jnp.min(low_ref[...]) < SKIP_MASS)
        def _():
            lax.fori_loop(2, nblk, tail_body, 0)


def _attn_prompt_call(q, k_all, v_all, j):
    bsz, s, d = q.shape
    tile = min(ATT_TILE, s)
    nblk = s // tile
    spec = pl.BlockSpec((1, s, LANES), lambda b, h: (b, 0, h))
    kv_spec = pl.BlockSpec((None, 1, s, LANES), lambda b, h: (j, b, 0, h))
    return pl.pallas_call(
        functools.partial(_attn_prompt_kernel, seq=s, tile=tile),
        grid=(bsz, d // LANES),
        in_specs=[spec, kv_spec, kv_spec],
        out_specs=spec,
        out_shape=jax.ShapeDtypeStruct((bsz, s, d), F32),
        scratch_shapes=[pltpu.VMEM((tile, tile), F32),
                        pltpu.VMEM((tile, tile), F32)]
                       + [pltpu.VMEM((2, tile, tile), F32)] * 6
                       + [pltpu.VMEM((2, tile, LANES), F32),
                          pltpu.VMEM((2, tile, LANES), F32),
                          pltpu.VMEM((nblk + 1, 2, tile, LANES), F32)],
        compiler_params=pltpu.CompilerParams(dimension_semantics=("arbitrary", "arbitrary"),
                                             vmem_limit_bytes=VMEM_LIMIT),
        name="sb_attn_prompt",
    )(q, k_all, v_all)


def _attn_sample_kernel(*refs, n_new, tile, blk, resume):
    if resume:
        q_ref, acc_in_ref, mass_in_ref, ck_ref, cv_ref, o_ref, acc_ref, mass_ref, qbd_ref = refs
    else:
        q_ref, kn_ref, vn_ref, ck_ref, cv_ref, o_ref, acc_ref, mass_ref, qbd_ref = refs
    t = pl.program_id(1)
    d = q_ref.shape[2]
    hd = d // N_HEADS
    m = N_HEADS * n_new
    tri = _tri_strict(tile)

    @pl.when(t == 0)
    def _():
        q = q_ref[0]
        col = lax.broadcasted_iota(jnp.int32, (n_new, d), 1)
        for h in range(N_HEADS):
            qbd_ref[h * n_new:(h + 1) * n_new, :] = jnp.where(col // hd == h, q, jnp.zeros_like(q))
        if resume:
            acc_ref[0] = acc_in_ref[0]
            mass_ref[0] = mass_in_ref[0]
        else:
            t_idx = lax.broadcasted_iota(jnp.int32, (m, tile), 0) % n_new
            s_idx = lax.broadcasted_iota(jnp.int32, (m, tile), 1)
            z = _dot_nt(qbd_ref[...], kn_ref[0])
            pv, cr = _sb_block(z, vn_ref[0], tri, jnp.zeros((m, 1), F32), s_idx < t_idx)
            acc_ref[0] = pv
            mass_ref[0] = jnp.broadcast_to(cr, (m, LANES))

    qbd = qbd_ref[...]
    for i in reversed(range(blk // tile)):
        kb = ck_ref[0, i * tile:(i + 1) * tile, :]
        vb = cv_ref[0, i * tile:(i + 1) * tile, :]
        pv, cr = _sb_block(_dot_nt(qbd, kb), vb, tri, mass_ref[0][:, 0:1], None)
        acc_ref[0] += pv
        mass_ref[0] = jnp.broadcast_to(cr, (m, LANES))

    @pl.when(t == pl.num_programs(1) - 1)
    def _():
        col = lax.broadcasted_iota(jnp.int32, (n_new, d), 1)
        o = jnp.zeros((n_new, d), F32)
        for h in range(N_HEADS):
            o = o + jnp.where(col // hd == h, acc_ref[0, h * n_new:(h + 1) * n_new, :], 0.0)
        o_ref[0] = o


def _attn_sample_call(q, first, caches, cache_spec, nkb, blk, resume):
    bsz, n_new, d = q.shape
    m = N_HEADS * n_new
    q_spec = pl.BlockSpec((1, n_new, d), lambda b, t: (b, 0, 0))
    acc_spec = pl.BlockSpec((1, m, d), lambda b, t: (b, 0, 0))
    mass_spec = pl.BlockSpec((1, m, LANES), lambda b, t: (b, 0, 0))
    if resume:
        first_specs = [acc_spec, mass_spec]
    else:
        first_specs = [pl.BlockSpec((1, ATT_TILE, d), lambda b, t: (b, 0, 0))] * 2
    return pl.pallas_call(
        functools.partial(_attn_sample_kernel, n_new=n_new, tile=ATT_TILE, blk=blk, resume=resume),
        grid=(bsz, nkb),
        in_specs=[q_spec] + first_specs + [cache_spec, cache_spec],
        out_specs=[q_spec, acc_spec, mass_spec],
        out_shape=[jax.ShapeDtypeStruct((bsz, n_new, d), F32),
                   jax.ShapeDtypeStruct((bsz, m, d), F32),
                   jax.ShapeDtypeStruct((bsz, m, LANES), F32)],
        scratch_shapes=[pltpu.VMEM((m, d), F32)],
        compiler_params=pltpu.CompilerParams(dimension_semantics=("arbitrary", "arbitrary"),
                                             vmem_limit_bytes=VMEM_LIMIT),
        name="sb_attn_sample",
    )(q, *first, *caches)


def _attn_sample(q, k_new, v_new, cache_k, cache_v, j):
    bsz, n_new, d = q.shape
    n_sb, _, past = cache_k.shape[:3]
    tile = ATT_TILE
    assert past % tile == 0 and n_new <= tile
    pad = ((0, 0), (0, tile - n_new), (0, 0))
    tail_k = cache_k[j, :, past - tile:].reshape(bsz, tile, d)
    tail_v = cache_v[j, :, past - tile:].reshape(bsz, tile, d)
    tail_spec = pl.BlockSpec((1, tile, d), lambda b, t: (b, 0, 0))
    o, acc, mass = _attn_sample_call(q, (jnp.pad(k_new, pad), jnp.pad(v_new, pad)),
                                     (tail_k, tail_v), tail_spec, 1, tile, False)
    n_rest = past // tile - 1
    if n_rest == 0:
        return o

    def older_keys(q, acc, mass, cache_k, cache_v):
        ck = cache_k.reshape(n_sb, bsz, past, d)
        cv = cache_v.reshape(n_sb, bsz, past, d)
        rest_spec = pl.BlockSpec((None, 1, tile, d), lambda b, t: (j, b, n_rest - 1 - t, 0))
        return _attn_sample_call(q, (acc, mass), (ck, cv), rest_spec, n_rest, tile, True)[0]

    return lax.cond(jnp.min(mass) < SKIP_MASS,
                    lambda o, *rest: older_keys(*rest), lambda o, *rest: o,
                    o, q, acc, mass, cache_k, cache_v)


def _heads_layout_kernel(x_ref, o_ref):
    tm, d = x_ref.shape
    o_ref[...] = x_ref[...].reshape(tm, N_HEADS, d // N_HEADS)


def _heads_layout_call(x):
    n, rows, d = x.shape
    tm = min(ROW_TILE, rows)
    hd = d // N_HEADS
    return pl.pallas_call(
        _heads_layout_kernel,
        grid=(n, rows // tm),
        in_specs=[pl.BlockSpec((None, tm, d), lambda a, i: (a, i, 0))],
        out_specs=pl.BlockSpec((None, tm, N_HEADS, hd), lambda a, i: (a, i, 0, 0)),
        out_shape=jax.ShapeDtypeStruct((n, rows, N_HEADS, hd), F32),
        compiler_params=pltpu.CompilerParams(dimension_semantics=("arbitrary", "arbitrary"),
                                             vmem_limit_bytes=VMEM_LIMIT),
        name="heads_layout",
    )(x)


def kernel(x_prompt, x_sample, cache_k, cache_v, state_pool, p_prompt, p_sample,
           ln_g, ln_b, ffn_w1, ffn_w3, ffn_w2, pool_w, pool_scale,
           sb_w_in, sb_w_out, ple_w_gate, ple_b_gate, ple_w_proj):
    depth = ln_g.shape[0]
    bp, sp, d = x_prompt.shape
    bs, ss, _ = x_sample.shape
    n_sb, _, past, heads, hd = cache_k.shape
    assert heads == N_HEADS and hd * heads == d and LANES == 2 * hd
    alpha = (2.0 * depth) ** 0.25
    sb_scale = 1.0 / math.sqrt(hd)

    w1 = ffn_w1.astype(BF16)
    w3 = ffn_w3.astype(BF16)
    w2 = ffn_w2.astype(BF16)
    pw = pool_w.astype(BF16)
    w_in = sb_w_in.astype(BF16)
    w_out = sb_w_out.astype(BF16)
    wg = ple_w_gate.astype(BF16)
    wp = ple_w_proj.astype(BF16)
    bg = ple_b_gate[:, None, :]
    scale = pool_scale[:, None, :]

    xp = x_prompt.reshape(bp * sp, d)
    xs = x_sample.reshape(bs * ss, d)
    pp = p_prompt.reshape(depth, bp * sp, -1)
    ps = p_sample.reshape(depth, bs * ss, -1)

    kp_all = vp_all = ks_all = vs_all = None
    pool_p, pool_s = [], []

    for i in range(depth):
        j = i // 2
        norms = [(ln_g, (i,)), (ln_b, (i,))]
        ffn0 = [(w1, (i, 0)), (w3, (i, 0)), (w2, (i, 0))]
        tail = norms + [(w1, (i, 1)), (w3, (i, 1)), (w2, (i, 1)), (wg, (i,)), (bg, (i,)), (wp, (i,))]
        if i % 2 == 0:
            xp1 = _pre_call(xp, ffn0 + norms, alpha)
            xs1 = _pre_call(xs, ffn0 + norms, alpha)
            pool = [(pw, (j,)), (scale, (j,))]
            xp, npp = _post_pool_call(xp1, sp, pp, i, pool + tail, alpha)
            prev_s = jnp.pad(state_pool[j], ((0, 0), (1, 0), (0, 0)))
            ms, nps = _pool_call(xs1.reshape(bs, ss, d), prev_s, *pool, past)
            xs = _post_call(xs1, ms.reshape(bs * ss, d), ps, i, None, tail, alpha)
            pool_p.append(npp)
            pool_s.append(nps)
        else:
            qkv = ffn0 + norms + [(w_in, (j,))]
            xp1, qp, kp_all, vp_all = _pre_qkv_call(xp, qkv, kp_all, vp_all, j, n_sb, alpha, sb_scale)
            xs1, qs, ks_all, vs_all = _pre_qkv_call(xs, qkv, ks_all, vs_all, j, n_sb, alpha, sb_scale)
            op = _attn_prompt_call(qp.reshape(bp, sp, d), kp_all.reshape(n_sb, bp, sp, d),
                                   vp_all.reshape(n_sb, bp, sp, d), j)
            osm = _attn_sample(qs.reshape(bs, ss, d), ks_all[j].reshape(bs, ss, d),
                               vs_all[j].reshape(bs, ss, d), cache_k, cache_v, j)
            xp = _post_call(xp1, op.reshape(bp * sp, d), pp, i, (w_out, (j,)), tail, alpha)
            xs = _post_call(xs1, osm.reshape(bs * ss, d), ps, i, (w_out, (j,)), tail, alpha)

    return (xp.reshape(bp, sp, d), xs.reshape(bs, ss, d),
            _heads_layout_call(kp_all).reshape(n_sb, bp, sp, heads, hd),
            _heads_layout_call(vp_all).reshape(n_sb, bp, sp, heads, hd),
            ks_all.reshape(n_sb, bs, ss, heads, hd), vs_all.reshape(n_sb, bs, ss, heads, hd),
            jnp.stack(pool_p), jnp.stack(pool_s))
```
